```python
import jax, jax.numpy as jnp
from jax import lax
import numpy as np

D_MODEL = 2048
BATCH = 1
SEQ = 16384
DEPTH = 1

PLE_DIM = 256
EPS = 1e-6
GM_CHUNK = 128
GM_GROUP_DIM = 128
GM_GROUPS = D_MODEL // 256
GM_WIDTH = GM_GROUPS * GM_GROUP_DIM
DN_HEAD_DIM = 128
DN_HEADS = D_MODEL // DN_HEAD_DIM
DN_WIDTH = DN_HEADS * DN_HEAD_DIM
DN_CONV = 4
DN_CHUNK = 64
D_FF = ((-(-(8 * D_MODEL) // 3) + 255) // 256) * 256
IN_SIZES = (2 * GM_WIDTH, 3 * DN_WIDTH, DN_WIDTH, DN_HEADS, DN_HEADS, 2 * D_MODEL)
IN_WIDTH = sum(IN_SIZES)
IN_SPLIT_POINTS = tuple(int(c) for c in np.cumsum(IN_SIZES)[:-1])

kernel_name = "hybrid_gmlp_gated_deltanet_block"


def rms_norm(x, gain):
    xf = x.astype(jnp.float32)
    xf = xf * lax.rsqrt(jnp.mean(xf * xf, axis=-1, keepdims=True) + EPS)
    return (xf * gain.astype(jnp.float32)).astype(x.dtype)


def l2norm(x):
    return x * lax.rsqrt(jnp.sum(x * x, axis=-1, keepdims=True) + EPS)


def chunked_spatial_gating(uv, v_gain, w_s, b_s):
    b, s, _ = uv.shape
    u, v = jnp.split(uv, 2, axis=-1)
    v = rms_norm(v, v_gain)
    v = v.reshape(b, s // GM_CHUNK, GM_CHUNK, GM_GROUPS, GM_GROUP_DIM)
    causal = jnp.tril(jnp.ones((GM_CHUNK, GM_CHUNK), dtype=bool))
    w = jnp.where(causal[None], w_s, 0.0)
    sv = jnp.einsum('gts,bnsgc->bntgc', w, v) + jnp.transpose(b_s)[None, None, :, :, None]
    return u * sv.reshape(b, s, GM_WIDTH)


def causal_short_conv(x, w):
    c = x.shape[-1]
    y = lax.conv_general_dilated(
        x, w[:, None, :].astype(x.dtype), window_strides=(1,),
        padding=((DN_CONV - 1, 0),), dimension_numbers=('NWC', 'WIO', 'NWC'),
        feature_group_count=c)
    return jax.nn.silu(y)


def gated_delta_rule(q, k, v, g, beta):
    out_dtype = v.dtype
    f32 = jnp.float32
    b, s, h, dk = q.shape
    dv = v.shape[-1]
    n = s // DN_CHUNK
    q = l2norm(q.astype(f32)) * (dk ** -0.5)
    k = l2norm(k.astype(f32))
    v = v.astype(f32)

    def chunks(t):
        t = t.reshape((b, n, DN_CHUNK, h) + t.shape[3:])
        return jnp.moveaxis(t, 3, 1)

    q, k, v = chunks(q), chunks(k), chunks(v)
    g = jnp.cumsum(chunks(g.astype(f32)), axis=-1)
    beta = chunks(beta.astype(f32))
    tril = jnp.tril(jnp.ones((DN_CHUNK, DN_CHUNK), dtype=bool))
    strict = jnp.tril(jnp.ones((DN_CHUNK, DN_CHUNK), dtype=bool), -1)
    diff = g[..., :, None] - g[..., None, :]
    decay = jnp.where(tril, jnp.exp(jnp.where(tril, diff, 0.0)), 0.0)
    k_beta = k * beta[..., None]
    m = jnp.where(strict, jnp.einsum('bhnid,bhnjd->bhnij', k_beta, k) * decay, 0.0)
    rhs = jnp.concatenate([v * beta[..., None], k_beta * jnp.exp(g)[..., None]], axis=-1)
    sol = lax.linalg.triangular_solve(m, rhs, left_side=True, lower=True, unit_diagonal=True)
    u_c, w_c = sol[..., :dv], sol[..., dv:]
    qk = jnp.einsum('bhnid,bhnjd->bhnij', q, k) * decay
    g_last = g[..., -1]
    q_dec = q * jnp.exp(g)[..., None]
    k_tail = k * jnp.exp(g_last[..., None] - g)[..., None]

    def step(state, xs):
        qk_i, u_i, w_i, qd_i, kt_i, gl_i = xs
        v_new = u_i - jnp.einsum('bhcd,bhde->bhce', w_i, state)
        o = jnp.einsum('bhcd,bhde->bhce', qd_i, state) + jnp.einsum('bhij,bhje->bhie', qk_i, v_new)
        state = state * jnp.exp(gl_i)[..., None, None] + jnp.einsum('bhcd,bhce->bhde', kt_i, v_new)
        return state, o

    xs = tuple(jnp.moveaxis(t, 2, 0) for t in (qk, u_c, w_c, q_dec, k_tail, g_last))
    state0 = jnp.zeros((b, h, dk, dv), f32)
    _, o = lax.scan(step, state0, xs)
    o = jnp.transpose(o, (1, 0, 3, 2, 4)).reshape(b, s, h, dv)
    return o.astype(out_dtype)


def setup_inputs(seed: int = 0) -> dict:
    key = jax.random.key(seed)
    ks = jax.random.split(key, 24)
    f32 = jnp.float32

    def normal(k, shape, scale):
        return jax.random.normal(k, shape, f32) * scale

    def gain(k, shape):
        return 1.0 + 0.02 * jax.random.normal(k, shape, f32)

    L = DEPTH
    x = jax.random.normal(ks[0], (BATCH, SEQ, D_MODEL), f32)
    p = jax.random.normal(ks[1], (DEPTH, BATCH, SEQ, PLE_DIM), f32)
    norm_mix = gain(ks[2], (L, D_MODEL))
    w_in = normal(ks[3], (L, D_MODEL, IN_WIDTH), D_MODEL ** -0.5)
    gm_v_norm = gain(ks[4], (L, GM_WIDTH))
    gm_w_s = normal(ks[5], (L, GM_GROUPS, GM_CHUNK, GM_CHUNK), GM_CHUNK ** -0.5)
    gm_b_s = 1.0 + 0.1 * jax.random.normal(ks[6], (L, GM_GROUPS, GM_CHUNK), f32)
    dn_conv_w = normal(ks[7], (L, DN_CONV, 3 * DN_WIDTH), DN_CONV ** -0.5)
    dn_a_log = jnp.log(jax.random.uniform(ks[8], (L, DN_HEADS), f32, 1.0, 16.0))
    dt = jnp.exp(jax.random.uniform(ks[9], (L, DN_HEADS), f32, np.log(1e-3), np.log(1e-1)))
    dn_dt_bias = dt + jnp.log(-jnp.expm1(-dt))
    dn_out_norm = gain(ks[10], (L, DN_HEAD_DIM))
    w_branch_a = normal(ks[11], (L, GM_WIDTH, D_MODEL), GM_WIDTH ** -0.5)
    w_branch_b = normal(ks[12], (L, DN_WIDTH, D_MODEL), DN_WIDTH ** -0.5)
    w_out = normal(ks[13], (L, D_MODEL, D_MODEL), D_MODEL ** -0.5)
    norm_ffn = gain(ks[14], (L, D_MODEL))
    w_gate_up = normal(ks[15], (L, D_MODEL, 2 * D_FF), D_MODEL ** -0.5)
    w_down = normal(ks[16], (L, D_FF, D_MODEL), D_FF ** -0.5)
    ple_norm = gain(ks[17], (L, D_MODEL))
    w_ple_gate = normal(ks[18], (L, D_MODEL, D_MODEL), D_MODEL ** -0.5)
    w_ple_proj = normal(ks[19], (L, PLE_DIM, D_MODEL), PLE_DIM ** -0.5)
    norm_final = gain(ks[20], (D_MODEL,))
    return {"x": x, "p": p, "norm_mix": norm_mix, "w_in": w_in, "gm_v_norm": gm_v_norm,
            "gm_w_s": gm_w_s, "gm_b_s": gm_b_s, "dn_conv_w": dn_conv_w, "dn_a_log": dn_a_log,
            "dn_dt_bias": dn_dt_bias, "dn_out_norm": dn_out_norm, "w_branch_a": w_branch_a,
            "w_branch_b": w_branch_b, "w_out": w_out, "norm_ffn": norm_ffn, "w_gate_up": w_gate_up,
            "w_down": w_down, "ple_norm": ple_norm, "w_ple_gate": w_ple_gate,
            "w_ple_proj": w_ple_proj, "norm_final": norm_final}


def reference(x, p, norm_mix, w_in, gm_v_norm, gm_w_s, gm_b_s, dn_conv_w, dn_a_log, dn_dt_bias,
              dn_out_norm, w_branch_a, w_branch_b, w_out, norm_ffn, w_gate_up, w_down,
              ple_norm, w_ple_gate, w_ple_proj, norm_final):
    b, s, _ = x.shape
    for i in range(DEPTH):
        h = rms_norm(x, norm_mix[i])
        proj = h @ w_in[i]
        gm_uv, dn_qkv, dn_z, dn_beta, dn_a, gate_logits = jnp.split(proj, IN_SPLIT_POINTS, axis=-1)
        y_a = chunked_spatial_gating(jax.nn.gelu(gm_uv, approximate=False), gm_v_norm[i], gm_w_s[i], gm_b_s[i])
        qkv = causal_short_conv(dn_qkv, dn_conv_w[i])
        q, k, v = jnp.split(qkv, 3, axis=-1)
        q = q.reshape(b, s, DN_HEADS, DN_HEAD_DIM)
        k = k.reshape(b, s, DN_HEADS, DN_HEAD_DIM)
        v = v.reshape(b, s, DN_HEADS, DN_HEAD_DIM)
        beta = jax.nn.sigmoid(dn_beta)
        g = -jnp.exp(dn_a_log[i].astype(jnp.float32)) * jax.nn.softplus(
            dn_a.astype(jnp.float32) + dn_dt_bias[i].astype(jnp.float32))
        o = gated_delta_rule(q, k, v, g, beta)
        o = rms_norm(o, dn_out_norm[i]) * jax.nn.silu(dn_z.reshape(b, s, DN_HEADS, DN_HEAD_DIM))
        y_b = o.reshape(b, s, DN_WIDTH)
        gate_a, gate_b = jnp.split(jax.nn.sigmoid(gate_logits), 2, axis=-1)
        merged = gate_a * (y_a @ w_branch_a[i]) + gate_b * (y_b @ w_branch_b[i])
        x = x + merged @ w_out[i]
        h = rms_norm(x, norm_ffn[i])
        f_gate, f_up = jnp.split(h @ w_gate_up[i], 2, axis=-1)
        x = x + (jax.nn.silu(f_gate) * f_up) @ w_down[i]
        ple_gate = jax.nn.sigmoid(rms_norm(x, ple_norm[i]) @ w_ple_gate[i])
        x = x + ple_gate * (p[i] @ w_ple_proj[i])
    return rms_norm(x, norm_final)
```

```python
import functools

import jax
import jax.numpy as jnp
from jax import lax
from jax.experimental import pallas as pl
from jax.experimental.pallas import tpu as pltpu

EPS = 1e-6
GM_CHUNK = 128
GM_GROUP_DIM = 128
DN_HEAD_DIM = 128
DN_CONV = 4
DN_CHUNK = 64
V7X_VMEM_LIMIT = 56 * 1024 * 1024

F32 = jnp.float32
BF16 = jnp.bfloat16


def _mm(a, b):
    return jnp.dot(a.astype(BF16), b.astype(BF16), preferred_element_type=F32)


def _mm_nt(a, b):
    return lax.dot_general(a.astype(BF16), b.astype(BF16), (((1,), (1,)), ((), ())),
                           preferred_element_type=F32)


def _mm_tn(a, b):
    return lax.dot_general(a.astype(BF16), b.astype(BF16), (((0,), (0,)), ((), ())),
                           preferred_element_type=F32)


def _rms(xf, gain):
    return xf * lax.rsqrt(jnp.mean(xf * xf, axis=-1, keepdims=True) + EPS) * gain


def _sigmoid(x):
    return 1.0 / (1.0 + jnp.exp(-x))


def _softplus(x):
    return jnp.maximum(x, 0.0) + jnp.log1p(jnp.exp(-jnp.abs(x)))


def _split3(x):
    x1 = x.astype(BF16)
    r1 = x - x1.astype(F32)
    x2 = r1.astype(BF16)
    x3 = (r1 - x2.astype(F32)).astype(BF16)
    return x1, x2, x3


def _params(sem, vmem=V7X_VMEM_LIMIT):
    return pltpu.CompilerParams(dimension_semantics=sem, vmem_limit_bytes=vmem)


def _norm_small_kernel(x_ref, gain_ref, wc_ref, wr_ref, alog_c_ref, dtb_c_ref, alog_r_ref, dtb_r_ref,
                       h_ref, bcol_ref, gcol_ref, brow_ref, grow_ref, *, heads):
    tm = x_ref.shape[0]
    h = _rms(x_ref[...], gain_ref[...]).astype(BF16)
    h_ref[...] = h
    pc = jnp.dot(h, wc_ref[...], preferred_element_type=F32)
    pr = lax.dot_general(wr_ref[...], h, (((1,), (1,)), ((), ())), preferred_element_type=F32)
    beta_c = _sigmoid(pc[:, :heads])
    g_c = -jnp.exp(alog_c_ref[...]) * _softplus(pc[:, heads:] + dtb_c_ref[...])
    beta_r = _sigmoid(pr[:heads, :])
    g_r = -jnp.exp(alog_r_ref[...]) * _softplus(pr[heads:, :] + dtb_r_ref[...])
    ri = lax.broadcasted_iota(jnp.int32, (tm, tm), 0)
    ci = lax.broadcasted_iota(jnp.int32, (tm, tm), 1)
    same = (ri // DN_CHUNK) == (ci // DN_CHUNK)
    ltri = jnp.where(same & (ci <= ri), 1.0, 0.0).astype(BF16)
    utri = jnp.where(same & (ri <= ci), 1.0, 0.0).astype(BF16)
    c1, c2, c3 = _split3(g_c)
    gc_c = (jnp.dot(ltri, c1, preferred_element_type=F32) + jnp.dot(ltri, c2, preferred_element_type=F32)
            + jnp.dot(ltri, c3, preferred_element_type=F32))
    r1, r2, r3 = _split3(g_r)
    gc_r = (jnp.dot(r1, utri, preferred_element_type=F32) + jnp.dot(r2, utri, preferred_element_type=F32)
            + jnp.dot(r3, utri, preferred_element_type=F32))
    bcol_ref[...] = beta_c
    gcol_ref[...] = gc_c
    brow_ref[...] = beta_r
    grow_ref[...] = gc_r


def _norm_small(x2d, norm_mix, w_ba, a_log, dt_bias, *, tm=512):
    s, d = x2d.shape
    heads = a_log.shape[0]
    wc = w_ba.astype(BF16)
    wr = jnp.transpose(w_ba).astype(BF16)
    full = lambda shape: pl.BlockSpec(shape, lambda i: (0,) * len(shape))
    return pl.pallas_call(
        functools.partial(_norm_small_kernel, heads=heads),
        grid=(s // tm,),
        in_specs=[pl.BlockSpec((tm, d), lambda i: (i, 0)), full((1, d)), full((d, 2 * heads)),
                  full((2 * heads, d)), full((1, heads)), full((1, heads)), full((heads, 1)), full((heads, 1))],
        out_specs=[pl.BlockSpec((tm, d), lambda i: (i, 0)),
                   pl.BlockSpec((tm, heads), lambda i: (i, 0)), pl.BlockSpec((tm, heads), lambda i: (i, 0)),
                   pl.BlockSpec((heads, tm), lambda i: (0, i)), pl.BlockSpec((heads, tm), lambda i: (0, i))],
        out_shape=[jax.ShapeDtypeStruct((s, d), BF16),
                   jax.ShapeDtypeStruct((s, heads), F32), jax.ShapeDtypeStruct((s, heads), F32),
                   jax.ShapeDtypeStruct((heads, s), F32), jax.ShapeDtypeStruct((heads, s), F32)],
        compiler_params=_params(("arbitrary",)),
        name="norm_small",
    )(x2d, norm_mix.reshape(1, d), wc, wr, a_log.reshape(1, heads), dt_bias.reshape(1, heads),
      a_log.reshape(heads, 1), dt_bias.reshape(heads, 1))


def _gmlp_kernel(h_ref, w_ref, vgain_ref, ws_ref, bs_ref, o_ref, *, groups):
    tm = h_ref.shape[0]
    gw = groups * GM_GROUP_DIM
    uv = jnp.dot(h_ref[...], w_ref[...], preferred_element_type=F32)
    uv = 0.5 * uv * (1.0 + lax.erf(uv * (2.0 ** -0.5)))
    u = uv[:, :gw]
    v = _rms(uv[:, gw:], vgain_ref[...]).astype(BF16)
    ri = lax.broadcasted_iota(jnp.int32, (GM_CHUNK, GM_CHUNK), 0)
    ci = lax.broadcasted_iota(jnp.int32, (GM_CHUNK, GM_CHUNK), 1)
    causal = ci <= ri
    for g in range(groups):
        wg = jnp.where(causal, ws_ref[g], 0.0).astype(BF16)
        bias = bs_ref[:, g:g + 1]
        cols = slice(g * GM_GROUP_DIM, (g + 1) * GM_GROUP_DIM)
        for c in range(tm // GM_CHUNK):
            rows = slice(c * GM_CHUNK, (c + 1) * GM_CHUNK)
            sv = jnp.dot(wg, v[rows, cols], preferred_element_type=F32) + bias
            o_ref[rows, cols] = (u[rows, cols] * sv).astype(o_ref.dtype)


def _gmlp(hb, w_uv, v_gain, w_s, b_s, *, tm=256):
    s, d = hb.shape
    groups = w_s.shape[0]
    gw = groups * GM_GROUP_DIM
    return pl.pallas_call(
        functools.partial(_gmlp_kernel, groups=groups),
        grid=(s // tm,),
        in_specs=[pl.BlockSpec((tm, d), lambda i: (i, 0)),
                  pl.BlockSpec((d, 2 * gw), lambda i: (0, 0)),
                  pl.BlockSpec((1, gw), lambda i: (0, 0)),
                  pl.BlockSpec((groups, GM_CHUNK, GM_CHUNK), lambda i: (0, 0, 0)),
                  pl.BlockSpec((GM_CHUNK, groups), lambda i: (0, 0))],
        out_specs=pl.BlockSpec((tm, gw), lambda i: (i, 0)),
        out_shape=jax.ShapeDtypeStruct((s, gw), BF16),
        compiler_params=_params(("arbitrary",)),
        name="gmlp",
    )(hb, w_uv, v_gain.reshape(1, gw), w_s, jnp.transpose(b_s))


def _proj_kernel(h_ref, w_ref, o_ref, *, n_plain, n_silu):
    j = pl.program_id(1)

    def acc():
        return jnp.dot(h_ref[...], w_ref[...], preferred_element_type=F32)

    @pl.when(j < n_plain)
    def _():
        o_ref[...] = acc().astype(o_ref.dtype)

    @pl.when((j >= n_plain) & (j < n_plain + n_silu))
    def _():
        a = acc()
        o_ref[...] = (a * _sigmoid(a)).astype(o_ref.dtype)

    @pl.when(j >= n_plain + n_silu)
    def _():
        o_ref[...] = _sigmoid(acc()).astype(o_ref.dtype)


def _proj(hb, w, *, n_plain_cols, n_silu_cols, tm=1024, tn=1024):
    s, d = hb.shape
    n = w.shape[1]
    return pl.pallas_call(
        functools.partial(_proj_kernel, n_plain=n_plain_cols // tn, n_silu=n_silu_cols // tn),
        grid=(s // tm, n // tn),
        in_specs=[pl.BlockSpec((tm, d), lambda i, j: (i, 0)), pl.BlockSpec((d, tn), lambda i, j: (0, j))],
        out_specs=pl.BlockSpec((tm, tn), lambda i, j: (i, j)),
        out_shape=jax.ShapeDtypeStruct((s, n), BF16),
        compiler_params=_params(("arbitrary", "arbitrary")),
        name="proj",
    )(hb, w)


def _delta_kernel(q_ref, k_ref, v_ref, z_ref, wq_ref, wk_ref, wv_ref, bcol_ref, gcol_ref, brow_ref, grow_ref,
                  onorm_ref, o_ref, state_ref, tail_ref, *, hb):
    tile = q_ref.shape[0]
    c = DN_CHUNK
    dh = DN_HEAD_DIM
    hg = pl.program_id(0)
    t = pl.program_id(1)

    @pl.when(t == 0)
    def _():
        state_ref[...] = jnp.zeros_like(state_ref)
        tail_ref[...] = jnp.zeros_like(tail_ref)

    def conv_silu(x_ref, w_ref, idx):
        x = x_ref[...].astype(F32)
        xp = jnp.concatenate([tail_ref[idx], x], axis=0)
        w = w_ref[...]
        y = x * w[DN_CONV - 1:DN_CONV, :]
        for j in range(DN_CONV - 1):
            off = 8 - (DN_CONV - 1) + j
            y = y + xp[off:off + tile, :] * w[j:j + 1, :]
        tail_ref[idx] = x[tile - 8:, :]
        return y * _sigmoid(y)

    q_all = conv_silu(q_ref, wq_ref, 0)
    k_all = conv_silu(k_ref, wk_ref, 1)
    v_all = conv_silu(v_ref, wv_ref, 2)

    ri = lax.broadcasted_iota(jnp.int32, (c, c), 0)
    ci = lax.broadcasted_iota(jnp.int32, (c, c), 1)
    tril = ci <= ri
    strict = ci < ri
    blk16 = (ri // 16) == (ci // 16)
    blk32 = (ri // 32) == (ci // 32)
    eye = jnp.where(ri == ci, 1.0, 0.0)
    lane_h = lax.broadcasted_iota(jnp.int32, bcol_ref.shape, 1)
    onorm = onorm_ref[...]

    for i in range(hb):
        head = hg * hb + i
        lanes = slice(i * dh, (i + 1) * dh)
        q = q_all[:, lanes]
        k = k_all[:, lanes]
        v = v_all[:, lanes]
        q = q * (lax.rsqrt(jnp.sum(q * q, axis=-1, keepdims=True) + EPS) * (dh ** -0.5))
        k = k * lax.rsqrt(jnp.sum(k * k, axis=-1, keepdims=True) + EPS)
        sel = lane_h == head
        b_col = jnp.sum(jnp.where(sel, bcol_ref[...], 0.0), axis=-1, keepdims=True)
        g_col = jnp.sum(jnp.where(sel, gcol_ref[...], 0.0), axis=-1, keepdims=True)
        b_row = brow_ref[pl.ds(head, 1), :]
        g_row = grow_ref[pl.ds(head, 1), :]
        state = state_ref[i]
        for n in range(tile // c):
            rows = slice(n * c, (n + 1) * c)
            qc, kc, vc = q[rows], k[rows], v[rows]
            gc, bc = g_col[rows], b_col[rows]
            gr, br = g_row[:, rows], b_row[:, rows]
            decay = jnp.where(tril, jnp.exp(jnp.where(tril, gc - gr, 0.0)), 0.0)
            kb = kc.astype(BF16)
            kk = _mm_nt(kb, kb)
            qk = _mm_nt(qc, kb) * decay
            nm = jnp.where(strict, -(kk * bc * decay), 0.0)
            dg = jnp.where(blk16, nm, 0.0)
            x = eye + dg
            d2 = _mm(dg, dg)
            r = _mm(d2, jnp.concatenate([x, d2], axis=1))
            x, d4 = x + r[:, :c], r[:, c:]
            r = _mm(d4, jnp.concatenate([x, d4], axis=1))
            x, d8 = x + r[:, :c], r[:, c:]
            x = x + _mm(d8, x)
            e1 = jnp.where(blk32 & jnp.logical_not(blk16), nm, 0.0)
            x = x + _mm(x, _mm(e1, x))
            e2 = jnp.where(blk32, 0.0, nm)
            x = x + _mm(x, _mm(e2, x))
            u_c = _mm(x * br, vc)
            w_c = _mm(x * (br * jnp.exp(gr)), kb)
            sb = state.astype(BF16)
            ws_qs = _mm(jnp.concatenate([w_c, qc], axis=0), sb)
            v_new = u_c - ws_qs[:c]
            o = ws_qs[c:] * jnp.exp(gc) + _mm(qk, v_new)
            g_last = gr[:, c - 1:c]
            state = state * jnp.exp(g_last) + _mm_tn(kb, v_new * jnp.exp(g_last - gc))
            o = _rms(o, onorm) * z_ref[rows, lanes].astype(F32)
            o_ref[rows, lanes] = o.astype(o_ref.dtype)
        state_ref[i] = state


def _delta(big, conv_w, bcol, gcol, brow, grow, out_norm, *, width, z_col0, hb=2, tile=256):
    s = big.shape[0]
    heads = bcol.shape[1]
    w = hb * DN_HEAD_DIM
    kq, kk, kv, kz = 0, width // w, 2 * width // w, z_col0 // w
    col = lambda off: pl.BlockSpec((tile, w), lambda g, t: (t, off + g))
    cw = lambda off: pl.BlockSpec((DN_CONV, w), lambda g, t: (0, off + g))
    return pl.pallas_call(
        functools.partial(_delta_kernel, hb=hb),
        grid=(heads // hb, s // tile),
        in_specs=[col(kq), col(kk), col(kv), col(kz), cw(kq), cw(kk), cw(kv),
                  pl.BlockSpec((tile, heads), lambda g, t: (t, 0)), pl.BlockSpec((tile, heads), lambda g, t: (t, 0)),
                  pl.BlockSpec((heads, tile), lambda g, t: (0, t)), pl.BlockSpec((heads, tile), lambda g, t: (0, t)),
                  pl.BlockSpec((1, DN_HEAD_DIM), lambda g, t: (0, 0))],
        out_specs=pl.BlockSpec((tile, w), lambda g, t: (t, g)),
        out_shape=jax.ShapeDtypeStruct((s, width), BF16),
        scratch_shapes=[pltpu.VMEM((hb, DN_HEAD_DIM, DN_HEAD_DIM), F32), pltpu.VMEM((3, 8, w), F32)],
        compiler_params=_params(("arbitrary", "arbitrary")),
        name="delta",
    )(big, big, big, big, conv_w, conv_w, conv_w, bcol, gcol, brow, grow, out_norm.reshape(1, DN_HEAD_DIM))


def _merge_kernel(ya_ref, wa_ref, yb_ref, wb_ref, ga_ref, gb_ref, o_ref):
    a = jnp.dot(ya_ref[...], wa_ref[...], preferred_element_type=F32)
    b = jnp.dot(yb_ref[...], wb_ref[...], preferred_element_type=F32)
    o_ref[...] = (ga_ref[...].astype(F32) * a + gb_ref[...].astype(F32) * b).astype(o_ref.dtype)


def _merge(ya, wa, yb, wb, big, *, gate_col0, tm=1024, tn=1024):
    s, ka = ya.shape
    kb = yb.shape[1]
    n = wa.shape[1]
    ga0 = gate_col0 // tn
    gb0 = (gate_col0 + n) // tn
    return pl.pallas_call(
        _merge_kernel,
        grid=(s // tm, n // tn),
        in_specs=[pl.BlockSpec((tm, ka), lambda i, j: (i, 0)), pl.BlockSpec((ka, tn), lambda i, j: (0, j)),
                  pl.BlockSpec((tm, kb), lambda i, j: (i, 0)), pl.BlockSpec((kb, tn), lambda i, j: (0, j)),
                  pl.BlockSpec((tm, tn), lambda i, j: (i, ga0 + j)), pl.BlockSpec((tm, tn), lambda i, j: (i, gb0 + j))],
        out_specs=pl.BlockSpec((tm, tn), lambda i, j: (i, j)),
        out_shape=jax.ShapeDtypeStruct((s, n), BF16),
        compiler_params=_params(("arbitrary", "arbitrary")),
        name="merge",
    )(ya, wa, yb, wb, big, big)


def _resid_mm_kernel(a_ref, w_ref, r_ref, o_ref):
    o_ref[...] = r_ref[...] + jnp.dot(a_ref[...], w_ref[...], preferred_element_type=F32)


def _resid_mm(a, w, resid, *, tm, tn, name):
    s, k = a.shape
    n = w.shape[1]
    return pl.pallas_call(
        _resid_mm_kernel,
        grid=(n // tn, s // tm),
        in_specs=[pl.BlockSpec((tm, k), lambda j, i: (i, 0)), pl.BlockSpec((k, tn), lambda j, i: (0, j)),
                  pl.BlockSpec((tm, tn), lambda j, i: (i, j))],
        out_specs=pl.BlockSpec((tm, tn), lambda j, i: (i, j)),
        out_shape=jax.ShapeDtypeStruct((s, n), F32),
        compiler_params=_params(("arbitrary", "arbitrary")),
        name=name,
    )(a, w, resid)


def _ffn_up_kernel(x_ref, gain_ref, wg_ref, wu_ref, o_ref, h_ref):
    @pl.when(pl.program_id(1) == 0)
    def _():
        h_ref[...] = _rms(x_ref[...], gain_ref[...]).astype(BF16)

    h = h_ref[...]
    g = jnp.dot(h, wg_ref[...], preferred_element_type=F32)
    u = jnp.dot(h, wu_ref[...], preferred_element_type=F32)
    o_ref[...] = (g * _sigmoid(g) * u).astype(o_ref.dtype)


def _ffn_up(x1, gain, w_gate_up, *, tm=1024, tn=512):
    s, d = x1.shape
    dff = w_gate_up.shape[1] // 2
    nj = dff // tn
    return pl.pallas_call(
        _ffn_up_kernel,
        grid=(s // tm, nj),
        in_specs=[pl.BlockSpec((tm, d), lambda i, j: (i, 0)), pl.BlockSpec((1, d), lambda i, j: (0, 0)),
                  pl.BlockSpec((d, tn), lambda i, j: (0, j)), pl.BlockSpec((d, tn), lambda i, j: (0, nj + j))],
        out_specs=pl.BlockSpec((tm, tn), lambda i, j: (i, j)),
        out_shape=jax.ShapeDtypeStruct((s, dff), BF16),
        scratch_shapes=[pltpu.VMEM((tm, d), BF16)],
        compiler_params=_params(("arbitrary", "arbitrary")),
        name="ffn_up",
    )(x1, gain.reshape(1, d), w_gate_up, w_gate_up)


def _ple_kernel(x_ref, p_ref, pgain_ref, wg_ref, wp_ref, fgain_ref, o_ref, *, final_norm):
    x = x_ref[...]
    gate = _sigmoid(jnp.dot(_rms(x, pgain_ref[...]).astype(BF16), wg_ref[...], preferred_element_type=F32))
    proj = jnp.dot(p_ref[...].astype(BF16), wp_ref[...], preferred_element_type=F32)
    y = x + gate * proj
    o_ref[...] = _rms(y, fgain_ref[...]) if final_norm else y


def _ple(x2, p2d, ple_norm, w_gate, w_proj, norm_final, *, final_norm, tm=512):
    s, d = x2.shape
    pd = p2d.shape[1]
    return pl.pallas_call(
        functools.partial(_ple_kernel, final_norm=final_norm),
        grid=(s // tm,),
        in_specs=[pl.BlockSpec((tm, d), lambda i: (i, 0)), pl.BlockSpec((tm, pd), lambda i: (i, 0)),
                  pl.BlockSpec((1, d), lambda i: (0, 0)), pl.BlockSpec((d, d), lambda i: (0, 0)),
                  pl.BlockSpec((pd, d), lambda i: (0, 0)), pl.BlockSpec((1, d), lambda i: (0, 0))],
        out_specs=pl.BlockSpec((tm, d), lambda i: (i, 0)),
        out_shape=jax.ShapeDtypeStruct((s, d), F32),
        compiler_params=_params(("arbitrary",)),
        name="ple",
    )(x2, p2d, ple_norm.reshape(1, d), w_gate, w_proj, norm_final.reshape(1, d))


def kernel(x, p, norm_mix, w_in, gm_v_norm, gm_w_s, gm_b_s, dn_conv_w, dn_a_log, dn_dt_bias, dn_out_norm,
           w_branch_a, w_branch_b, w_out, norm_ffn, w_gate_up, w_down, ple_norm, w_ple_gate, w_ple_proj,
           norm_final):
    b, s, d = x.shape
    depth = w_in.shape[0]
    gm_w = w_branch_a.shape[1]
    dn_w = w_branch_b.shape[1]
    heads = dn_a_log.shape[1]
    c_uv, c_qkv, c_z = 0, 2 * gm_w, 2 * gm_w + 3 * dn_w
    c_ba = c_z + dn_w
    c_gate = c_ba + 2 * heads
    xs = x.reshape(b * s, d)
    for i in range(depth):
        wi = w_in[i]
        w_uv = wi[:, c_uv:c_qkv].astype(BF16)
        w_big = jnp.concatenate([wi[:, c_qkv:c_ba], wi[:, c_gate:]], axis=1).astype(BF16)
        hb16, bcol, gcol, brow, grow = _norm_small(xs, norm_mix[i], wi[:, c_ba:c_gate], dn_a_log[i], dn_dt_bias[i])
        y_a = _gmlp(hb16, w_uv, gm_v_norm[i], gm_w_s[i], gm_b_s[i])
        big = _proj(hb16, w_big, n_plain_cols=3 * dn_w, n_silu_cols=dn_w)
        y_b = _delta(big, dn_conv_w[i], bcol, gcol, brow, grow, dn_out_norm[i], width=dn_w, z_col0=3 * dn_w)
        merged = _merge(y_a, w_branch_a[i].astype(BF16), y_b, w_branch_b[i].astype(BF16), big,
                        gate_col0=4 * dn_w)
        x1 = _resid_mm(merged, w_out[i].astype(BF16), xs, tm=1024, tn=1024, name="out_proj")
        f = _ffn_up(x1, norm_ffn[i], w_gate_up[i].astype(BF16))
        x2 = _resid_mm(f, w_down[i].astype(BF16), x1, tm=512, tn=1024, name="ffn_down")
        xs = _ple(x2, p[i].reshape(b * s, -1), ple_norm[i], w_ple_gate[i].astype(BF16),
                  w_ple_proj[i].astype(BF16), norm_final, final_norm=(i == depth - 1))
    return xs.reshape(b, s, d)
```

```python
import functools

import jax
import jax.numpy as jnp
from jax import lax
from jax.experimental import pallas as pl
from jax.experimental.pallas import tpu as pltpu

EPS = 1e-6
GM_CHUNK = 128
GM_GROUP_DIM = 128
DN_HEAD_DIM = 128
DN_CONV = 4
DN_CHUNK = 64
V7X_VMEM_LIMIT = 56 * 1024 * 1024

F32 = jnp.float32
BF16 = jnp.bfloat16


def _mm(a, b):
    return jnp.dot(a.astype(BF16), b.astype(BF16), preferred_element_type=F32)


def _mm_nt(a, b):
    return lax.dot_general(a.astype(BF16), b.astype(BF16), (((1,), (1,)), ((), ())),
                           preferred_element_type=F32)


def _mm_tn(a, b):
    return lax.dot_general(a.astype(BF16), b.astype(BF16), (((0,), (0,)), ((), ())),
                           preferred_element_type=F32)


def _rms(xf, gain):
    return xf * lax.rsqrt(jnp.mean(xf * xf, axis=-1, keepdims=True) + EPS) * gain


def _sigmoid(x):
    return 1.0 / (1.0 + jnp.exp(-x))


def _softplus(x):
    return jnp.maximum(x, 0.0) + jnp.log1p(jnp.exp(-jnp.abs(x)))


def _split3(x):
    x1 = x.astype(BF16)
    r1 = x - x1.astype(F32)
    x2 = r1.astype(BF16)
    x3 = (r1 - x2.astype(F32)).astype(BF16)
    return x1, x2, x3


def _params(sem, vmem=V7X_VMEM_LIMIT):
    return pltpu.CompilerParams(dimension_semantics=sem, vmem_limit_bytes=vmem)


def _norm_small_kernel(x_ref, gain_ref, wc_ref, wr_ref, alog_c_ref, dtb_c_ref, alog_r_ref, dtb_r_ref,
                       h_ref, bcol_ref, gcol_ref, brow_ref, grow_ref, *, heads):
    tm = x_ref.shape[0]
    h = _rms(x_ref[...], gain_ref[...]).astype(BF16)
    h_ref[...] = h
    pc = jnp.dot(h, wc_ref[...], preferred_element_type=F32)
    pr = lax.dot_general(wr_ref[...], h, (((1,), (1,)), ((), ())), preferred_element_type=F32)
    beta_c = _sigmoid(pc[:, :heads])
    g_c = -jnp.exp(alog_c_ref[...]) * _softplus(pc[:, heads:] + dtb_c_ref[...])
    beta_r = _sigmoid(pr[:heads, :])
    g_r = -jnp.exp(alog_r_ref[...]) * _softplus(pr[heads:, :] + dtb_r_ref[...])
    ri = lax.broadcasted_iota(jnp.int32, (tm, tm), 0)
    ci = lax.broadcasted_iota(jnp.int32, (tm, tm), 1)
    same = (ri // DN_CHUNK) == (ci // DN_CHUNK)
    ltri = jnp.where(same & (ci <= ri), 1.0, 0.0).astype(BF16)
    utri = jnp.where(same & (ri <= ci), 1.0, 0.0).astype(BF16)
    c1, c2, c3 = _split3(g_c)
    gc_c = (jnp.dot(ltri, c1, preferred_element_type=F32) + jnp.dot(ltri, c2, preferred_element_type=F32)
            + jnp.dot(ltri, c3, preferred_element_type=F32))
    r1, r2, r3 = _split3(g_r)
    gc_r = (jnp.dot(r1, utri, preferred_element_type=F32) + jnp.dot(r2, utri, preferred_element_type=F32)
            + jnp.dot(r3, utri, preferred_element_type=F32))
    bcol_ref[...] = beta_c
    gcol_ref[...] = gc_c
    brow_ref[...] = beta_r
    grow_ref[...] = gc_r


def _norm_small(x2d, norm_mix, w_ba, a_log, dt_bias, *, tm=512):
    s, d = x2d.shape
    heads = a_log.shape[0]
    wc = w_ba.astype(BF16)
    wr = jnp.transpose(w_ba).astype(BF16)
    full = lambda shape: pl.BlockSpec(shape, lambda i: (0,) * len(shape))
    return pl.pallas_call(
        functools.partial(_norm_small_kernel, heads=heads),
        grid=(s // tm,),
        in_specs=[pl.BlockSpec((tm, d), lambda i: (i, 0)), full((1, d)), full((d, 2 * heads)),
                  full((2 * heads, d)), full((1, heads)), full((1, heads)), full((heads, 1)), full((heads, 1))],
        out_specs=[pl.BlockSpec((tm, d), lambda i: (i, 0)),
                   pl.BlockSpec((tm, heads), lambda i: (i, 0)), pl.BlockSpec((tm, heads), lambda i: (i, 0)),
                   pl.BlockSpec((heads, tm), lambda i: (0, i)), pl.BlockSpec((heads, tm), lambda i: (0, i))],
        out_shape=[jax.ShapeDtypeStruct((s, d), BF16),
                   jax.ShapeDtypeStruct((s, heads), F32), jax.ShapeDtypeStruct((s, heads), F32),
                   jax.ShapeDtypeStruct((heads, s), F32), jax.ShapeDtypeStruct((heads, s), F32)],
        compiler_params=_params(("arbitrary",)),
        name="norm_small",
    )(x2d, norm_mix.reshape(1, d), wc, wr, a_log.reshape(1, heads), dt_bias.reshape(1, heads),
      a_log.reshape(heads, 1), dt_bias.reshape(heads, 1))


def _gmlp_kernel(h_ref, w_ref, vgain_ref, ws_ref, bs_ref, o_ref, *, groups):
    tm = h_ref.shape[0]
    gw = groups * GM_GROUP_DIM
    uv = jnp.dot(h_ref[...], w_ref[...], preferred_element_type=F32)
    uv = 0.5 * uv * (1.0 + lax.erf(uv * (2.0 ** -0.5)))
    u = uv[:, :gw]
    v = _rms(uv[:, gw:], vgain_ref[...]).astype(BF16)
    ri = lax.broadcasted_iota(jnp.int32, (GM_CHUNK, GM_CHUNK), 0)
    ci = lax.broadcasted_iota(jnp.int32, (GM_CHUNK, GM_CHUNK), 1)
    causal = ci <= ri
    for g in range(groups):
        wg = jnp.where(causal, ws_ref[g], 0.0).astype(BF16)
        bias = bs_ref[:, g:g + 1]
        cols = slice(g * GM_GROUP_DIM, (g + 1) * GM_GROUP_DIM)
        for c in range(tm // GM_CHUNK):
            rows = slice(c * GM_CHUNK, (c + 1) * GM_CHUNK)
            sv = jnp.dot(wg, v[rows, cols], preferred_element_type=F32) + bias
            o_ref[rows, cols] = (u[rows, cols] * sv).astype(o_ref.dtype)


def _gmlp(hb, w_uv, v_gain, w_s, b_s, *, tm=256):
    s, d = hb.shape
    groups = w_s.shape[0]
    gw = groups * GM_GROUP_DIM
    return pl.pallas_call(
        functools.partial(_gmlp_kernel, groups=groups),
        grid=(s // tm,),
        in_specs=[pl.BlockSpec((tm, d), lambda i: (i, 0)),
                  pl.BlockSpec((d, 2 * gw), lambda i: (0, 0)),
                  pl.BlockSpec((1, gw), lambda i: (0, 0)),
                  pl.BlockSpec((groups, GM_CHUNK, GM_CHUNK), lambda i: (0, 0, 0)),
                  pl.BlockSpec((GM_CHUNK, groups), lambda i: (0, 0))],
        out_specs=pl.BlockSpec((tm, gw), lambda i: (i, 0)),
        out_shape=jax.ShapeDtypeStruct((s, gw), BF16),
        compiler_params=_params(("arbitrary",)),
        name="gmlp",
    )(hb, w_uv, v_gain.reshape(1, gw), w_s, jnp.transpose(b_s))


def _proj_kernel(h_ref, w_ref, o_ref, *, n_plain, n_silu):
    j = pl.program_id(1)

    def acc():
        return jnp.dot(h_ref[...], w_ref[...], preferred_element_type=F32)

    @pl.when(j < n_plain)
    def _():
        o_ref[...] = acc().astype(o_ref.dtype)

    @pl.when((j >= n_plain) & (j < n_plain + n_silu))
    def _():
        a = acc()
        o_ref[...] = (a * _sigmoid(a)).astype(o_ref.dtype)

    @pl.when(j >= n_plain + n_silu)
    def _():
        o_ref[...] = _sigmoid(acc()).astype(o_ref.dtype)


def _proj(hb, w, *, n_plain_cols, n_silu_cols, tm=1024, tn=1024):
    s, d = hb.shape
    n = w.shape[1]
    return pl.pallas_call(
        functools.partial(_proj_kernel, n_plain=n_plain_cols // tn, n_silu=n_silu_cols // tn),
        grid=(s // tm, n // tn),
        in_specs=[pl.BlockSpec((tm, d), lambda i, j: (i, 0)), pl.BlockSpec((d, tn), lambda i, j: (0, j))],
        out_specs=pl.BlockSpec((tm, tn), lambda i, j: (i, j)),
        out_shape=jax.ShapeDtypeStruct((s, n), BF16),
        compiler_params=_params(("arbitrary", "arbitrary")),
        name="proj",
    )(hb, w)


def _delta_kernel(q_ref, k_ref, v_ref, z_ref, wq_ref, wk_ref, wv_ref, bcol_ref, gcol_ref, brow_ref, grow_ref,
                  onorm_ref, o_ref, state_ref, tail_ref, *, hb):
    tile = q_ref.shape[0]
    c = DN_CHUNK
    dh = DN_HEAD_DIM
    hg = pl.program_id(0)
    t = pl.program_id(1)

    @pl.when(t == 0)
    def _():
        state_ref[...] = jnp.zeros_like(state_ref)
        tail_ref[...] = jnp.zeros_like(tail_ref)

    def conv_silu(x_ref, w_ref, idx):
        x = x_ref[...].astype(F32)
        xp = jnp.concatenate([tail_ref[idx], x], axis=0)
        w = w_ref[...]
        y = x * w[DN_CONV - 1:DN_CONV, :]
        for j in range(DN_CONV - 1):
            off = 8 - (DN_CONV - 1) + j
            y = y + xp[off:off + tile, :] * w[j:j + 1, :]
        tail_ref[idx] = x[tile - 8:, :]
        return y * _sigmoid(y)

    q_all = conv_silu(q_ref, wq_ref, 0)
    k_all = conv_silu(k_ref, wk_ref, 1)
    v_all = conv_silu(v_ref, wv_ref, 2)

    ri = lax.broadcasted_iota(jnp.int32, (c, c), 0)
    ci = lax.broadcasted_iota(jnp.int32, (c, c), 1)
    tril = ci <= ri
    strict = ci < ri
    blk16 = (ri // 16) == (ci // 16)
    blk32 = (ri // 32) == (ci // 32)
    eye = jnp.where(ri == ci, 1.0, 0.0)
    lane_h = lax.broadcasted_iota(jnp.int32, bcol_ref.shape, 1)
    onorm = onorm_ref[...]

    nch = tile // c
    units = [(i, n) for i in range(hb) for n in range(nch)]
    nu = len(units)
    heads_of = [hg * hb + i for i in range(hb)]

    def rows_of(n):
        return slice(n * c, (n + 1) * c)

    qn, kn, vv, b_col, g_col, b_row, g_row = [], [], [], [], [], [], []
    for i in range(hb):
        lanes = slice(i * dh, (i + 1) * dh)
        q = q_all[:, lanes]
        k = k_all[:, lanes]
        qn.append((q * (lax.rsqrt(jnp.sum(q * q, axis=-1, keepdims=True) + EPS) * (dh ** -0.5))).astype(BF16))
        kn.append((k * lax.rsqrt(jnp.sum(k * k, axis=-1, keepdims=True) + EPS)).astype(BF16))
        vv.append(v_all[:, lanes].astype(BF16))
        sel = lane_h == heads_of[i]
        b_col.append(jnp.sum(jnp.where(sel, bcol_ref[...], 0.0), axis=-1, keepdims=True))
        g_col.append(jnp.sum(jnp.where(sel, gcol_ref[...], 0.0), axis=-1, keepdims=True))
        b_row.append(brow_ref[pl.ds(heads_of[i], 1), :])
        g_row.append(grow_ref[pl.ds(heads_of[i], 1), :])

    kb = [kn[i][rows_of(n)] for i, n in units]
    qb = [qn[i][rows_of(n)] for i, n in units]
    vb = [vv[i][rows_of(n)] for i, n in units]
    gc = [g_col[i][rows_of(n)] for i, n in units]
    bc = [b_col[i][rows_of(n)] for i, n in units]
    gr = [g_row[i][:, rows_of(n)] for i, n in units]
    br = [b_row[i][:, rows_of(n)] for i, n in units]
    decay = [jnp.where(tril, jnp.exp(jnp.where(tril, gc[u] - gr[u], 0.0)), 0.0) for u in range(nu)]
    kq = [_mm_nt(jnp.concatenate([kb[u], qb[u]], axis=0), kb[u]) for u in range(nu)]
    qk = [(kq[u][c:] * decay[u]).astype(BF16) for u in range(nu)]
    nm = [jnp.where(strict, -(kq[u][:c] * bc[u] * decay[u]), 0.0) for u in range(nu)]
    dg = [jnp.where(blk16, nm[u], 0.0) for u in range(nu)]
    x = [eye + dg[u] for u in range(nu)]
    dk = [_mm(dg[u], dg[u]).astype(BF16) for u in range(nu)]
    for _ in range(2):
        r = [_mm(jnp.concatenate([x[u].astype(BF16), dk[u]], axis=0), dk[u]) for u in range(nu)]
        x = [x[u] + r[u][:c] for u in range(nu)]
        dk = [r[u][c:].astype(BF16) for u in range(nu)]
    x = [x[u] + _mm(x[u], dk[u]) for u in range(nu)]
    for level in (0, 1):
        if level == 0:
            e = [jnp.where(blk32 & jnp.logical_not(blk16), nm[u], 0.0) for u in range(nu)]
        else:
            e = [jnp.where(blk32, 0.0, nm[u]) for u in range(nu)]
        y = [_mm(e[u], x[u]) for u in range(nu)]
        x = [x[u] + _mm(x[u], y[u]) for u in range(nu)]
    u_c = [_mm(x[u] * br[u], vb[u]) for u in range(nu)]
    w_c = [_mm(x[u] * (br[u] * jnp.exp(gr[u])), kb[u]).astype(BF16) for u in range(nu)]
    eg = [jnp.exp(gc[u]) for u in range(nu)]
    g_last = [gr[u][:, c - 1:c] for u in range(nu)]
    eg_tail = [jnp.exp(g_last[u] - gc[u]) for u in range(nu)]
    eg_last = [jnp.exp(g_last[u]) for u in range(nu)]

    state = [state_ref[i] for i in range(hb)]
    for n in range(nch):
        us = [i * nch + n for i in range(hb)]
        ws_qs = [_mm(jnp.concatenate([w_c[u], qb[u]], axis=0), state[i]) for i, u in enumerate(us)]
        v_new = [u_c[u] - ws_qs[i][:c] for i, u in enumerate(us)]
        o = [ws_qs[i][c:] * eg[u] + _mm(qk[u], v_new[i]) for i, u in enumerate(us)]
        state = [state[i] * eg_last[u] + _mm_tn(kb[u], v_new[i] * eg_tail[u]) for i, u in enumerate(us)]
        for i in range(hb):
            lanes = slice(i * dh, (i + 1) * dh)
            out = _rms(o[i], onorm) * z_ref[rows_of(n), lanes].astype(F32)
            o_ref[rows_of(n), lanes] = out.astype(o_ref.dtype)
    for i in range(hb):
        state_ref[i] = state[i]


def _delta(big, conv_w, bcol, gcol, brow, grow, out_norm, *, width, z_col0, hb=4, tile=256):
    s = big.shape[0]
    heads = bcol.shape[1]
    w = hb * DN_HEAD_DIM
    kq, kk, kv, kz = 0, width // w, 2 * width // w, z_col0 // w
    col = lambda off: pl.BlockSpec((tile, w), lambda g, t: (t, off + g))
    cw = lambda off: pl.BlockSpec((DN_CONV, w), lambda g, t: (0, off + g))
    return pl.pallas_call(
        functools.partial(_delta_kernel, hb=hb),
        grid=(heads // hb, s // tile),
        in_specs=[col(kq), col(kk), col(kv), col(kz), cw(kq), cw(kk), cw(kv),
                  pl.BlockSpec((tile, heads), lambda g, t: (t, 0)), pl.BlockSpec((tile, heads), lambda g, t: (t, 0)),
                  pl.BlockSpec((heads, tile), lambda g, t: (0, t)), pl.BlockSpec((heads, tile), lambda g, t: (0, t)),
                  pl.BlockSpec((1, DN_HEAD_DIM), lambda g, t: (0, 0))],
        out_specs=pl.BlockSpec((tile, w), lambda g, t: (t, g)),
        out_shape=jax.ShapeDtypeStruct((s, width), BF16),
        scratch_shapes=[pltpu.VMEM((hb, DN_HEAD_DIM, DN_HEAD_DIM), F32), pltpu.VMEM((3, 8, w), F32)],
        compiler_params=_params(("arbitrary", "arbitrary")),
        name="delta",
    )(big, big, big, big, conv_w, conv_w, conv_w, bcol, gcol, brow, grow, out_norm.reshape(1, DN_HEAD_DIM))


def _merge_kernel(ya_ref, wa_ref, yb_ref, wb_ref, ga_ref, gb_ref, o_ref):
    a = jnp.dot(ya_ref[...], wa_ref[...], preferred_element_type=F32)
    b = jnp.dot(yb_ref[...], wb_ref[...], preferred_element_type=F32)
    o_ref[...] = (ga_ref[...].astype(F32) * a + gb_ref[...].astype(F32) * b).astype(o_ref.dtype)


def _merge(ya, wa, yb, wb, big, *, gate_col0, tm=1024, tn=1024):
    s, ka = ya.shape
    kb = yb.shape[1]
    n = wa.shape[1]
    ga0 = gate_col0 // tn
    gb0 = (gate_col0 + n) // tn
    return pl.pallas_call(
        _merge_kernel,
        grid=(s // tm, n // tn),
        in_specs=[pl.BlockSpec((tm, ka), lambda i, j: (i, 0)), pl.BlockSpec((ka, tn), lambda i, j: (0, j)),
                  pl.BlockSpec((tm, kb), lambda i, j: (i, 0)), pl.BlockSpec((kb, tn), lambda i, j: (0, j)),
                  pl.BlockSpec((tm, tn), lambda i, j: (i, ga0 + j)), pl.BlockSpec((tm, tn), lambda i, j: (i, gb0 + j))],
        out_specs=pl.BlockSpec((tm, tn), lambda i, j: (i, j)),
        out_shape=jax.ShapeDtypeStruct((s, n), BF16),
        compiler_params=_params(("arbitrary", "arbitrary")),
        name="merge",
    )(ya, wa, yb, wb, big, big)


def _resid_mm_kernel(a_ref, w_ref, r_ref, o_ref):
    o_ref[...] = r_ref[...] + jnp.dot(a_ref[...], w_ref[...], preferred_element_type=F32)


def _resid_mm(a, w, resid, *, tm, tn, name):
    s, k = a.shape
    n = w.shape[1]
    return pl.pallas_call(
        _resid_mm_kernel,
        grid=(n // tn, s // tm),
        in_specs=[pl.BlockSpec((tm, k), lambda j, i: (i, 0)), pl.BlockSpec((k, tn), lambda j, i: (0, j)),
                  pl.BlockSpec((tm, tn), lambda j, i: (i, j))],
        out_specs=pl.BlockSpec((tm, tn), lambda j, i: (i, j)),
        out_shape=jax.ShapeDtypeStruct((s, n), F32),
        compiler_params=_params(("arbitrary", "arbitrary")),
        name=name,
    )(a, w, resid)


def _ffn_up_kernel(x_ref, gain_ref, wg_ref, wu_ref, o_ref, h_ref):
    @pl.when(pl.program_id(1) == 0)
    def _():
        h_ref[...] = _rms(x_ref[...], gain_ref[...]).astype(BF16)

    h = h_ref[...]
    g = jnp.dot(h, wg_ref[...], preferred_element_type=F32)
    u = jnp.dot(h, wu_ref[...], preferred_element_type=F32)
    o_ref[...] = (g * _sigmoid(g) * u).astype(o_ref.dtype)


def _ffn_up(x1, gain, w_gate_up, *, tm=1024, tn=512):
    s, d = x1.shape
    dff = w_gate_up.shape[1] // 2
    nj = dff // tn
    return pl.pallas_call(
        _ffn_up_kernel,
        grid=(s // tm, nj),
        in_specs=[pl.BlockSpec((tm, d), lambda i, j: (i, 0)), pl.BlockSpec((1, d), lambda i, j: (0, 0)),
                  pl.BlockSpec((d, tn), lambda i, j: (0, j)), pl.BlockSpec((d, tn), lambda i, j: (0, nj + j))],
        out_specs=pl.BlockSpec((tm, tn), lambda i, j: (i, j)),
        out_shape=jax.ShapeDtypeStruct((s, dff), BF16),
        scratch_shapes=[pltpu.VMEM((tm, d), BF16)],
        compiler_params=_params(("arbitrary", "arbitrary")),
        name="ffn_up",
    )(x1, gain.reshape(1, d), w_gate_up, w_gate_up)


def _ple_kernel(x_ref, p_ref, pgain_ref, wg_ref, wp_ref, fgain_ref, o_ref, *, final_norm):
    x = x_ref[...]
    gate = _sigmoid(jnp.dot(_rms(x, pgain_ref[...]).astype(BF16), wg_ref[...], preferred_element_type=F32))
    proj = jnp.dot(p_ref[...].astype(BF16), wp_ref[...], preferred_element_type=F32)
    y = x + gate * proj
    o_ref[...] = _rms(y, fgain_ref[...]) if final_norm else y


def _ple(x2, p2d, ple_norm, w_gate, w_proj, norm_final, *, final_norm, tm=512):
    s, d = x2.shape
    pd = p2d.shape[1]
    return pl.pallas_call(
        functools.partial(_ple_kernel, final_norm=final_norm),
        grid=(s // tm,),
        in_specs=[pl.BlockSpec((tm, d), lambda i: (i, 0)), pl.BlockSpec((tm, pd), lambda i: (i, 0)),
                  pl.BlockSpec((1, d), lambda i: (0, 0)), pl.BlockSpec((d, d), lambda i: (0, 0)),
                  pl.BlockSpec((pd, d), lambda i: (0, 0)), pl.BlockSpec((1, d), lambda i: (0, 0))],
        out_specs=pl.BlockSpec((tm, d), lambda i: (i, 0)),
        out_shape=jax.ShapeDtypeStruct((s, d), F32),
        compiler_params=_params(("arbitrary",)),
        name="ple",
    )(x2, p2d, ple_norm.reshape(1, d), w_gate, w_proj, norm_final.reshape(1, d))


def kernel(x, p, norm_mix, w_in, gm_v_norm, gm_w_s, gm_b_s, dn_conv_w, dn_a_log, dn_dt_bias, dn_out_norm,
           w_branch_a, w_branch_b, w_out, norm_ffn, w_gate_up, w_down, ple_norm, w_ple_gate, w_ple_proj,
           norm_final):
    b, s, d = x.shape
    depth = w_in.shape[0]
    gm_w = w_branch_a.shape[1]
    dn_w = w_branch_b.shape[1]
    heads = dn_a_log.shape[1]
    c_uv, c_qkv, c_z = 0, 2 * gm_w, 2 * gm_w + 3 * dn_w
    c_ba = c_z + dn_w
    c_gate = c_ba + 2 * heads
    xs = x.reshape(b * s, d)
    for i in range(depth):
        wi = w_in[i]
        w_uv = wi[:, c_uv:c_qkv].astype(BF16)
        w_big = jnp.concatenate([wi[:, c_qkv:c_ba], wi[:, c_gate:]], axis=1).astype(BF16)
        hb16, bcol, gcol, brow, grow = _norm_small(xs, norm_mix[i], wi[:, c_ba:c_gate], dn_a_log[i], dn_dt_bias[i])
        y_a = _gmlp(hb16, w_uv, gm_v_norm[i], gm_w_s[i], gm_b_s[i])
        big = _proj(hb16, w_big, n_plain_cols=3 * dn_w, n_silu_cols=dn_w)
        y_b = _delta(big, dn_conv_w[i], bcol, gcol, brow, grow, dn_out_norm[i], width=dn_w, z_col0=3 * dn_w)
        merged = _merge(y_a, w_branch_a[i].astype(BF16), y_b, w_branch_b[i].astype(BF16), big,
                        gate_col0=4 * dn_w)
        x1 = _resid_mm(merged, w_out[i].astype(BF16), xs, tm=1024, tn=1024, name="out_proj")
        f = _ffn_up(x1, norm_ffn[i], w_gate_up[i].astype(BF16))
        x2 = _resid_mm(f, w_down[i].astype(BF16), x1, tm=512, tn=1024, name="ffn_down")
        xs = _ple(x2, p[i].reshape(b * s, -1), ple_norm[i], w_ple_gate[i].astype(BF16),
                  w_ple_proj[i].astype(BF16), norm_final, final_norm=(i == depth - 1))
    return xs.reshape(b, s, d)
```

```python
import functools

import jax
import jax.numpy as jnp
from jax import lax
from jax.experimental import pallas as pl
from jax.experimental.pallas import tpu as pltpu

EPS = 1e-6
GM_CHUNK = 128
GM_GROUP_DIM = 128
DN_HEAD_DIM = 128
DN_CONV = 4
DN_CHUNK = 64
MXU_COLS = 256
V7X_VMEM_LIMIT = 56 * 1024 * 1024

F32 = jnp.float32
BF16 = jnp.bfloat16


def _mm(a, b):
    return jnp.dot(a.astype(BF16), b.astype(BF16), preferred_element_type=F32)


def _mm_nt(a, b):
    return lax.dot_general(a.astype(BF16), b.astype(BF16), (((1,), (1,)), ((), ())),
                           preferred_element_type=F32)


def _mm_tn(a, b):
    return lax.dot_general(a.astype(BF16), b.astype(BF16), (((0,), (0,)), ((), ())),
                           preferred_element_type=F32)


def _rms(xf, gain):
    return xf * lax.rsqrt(jnp.mean(xf * xf, axis=-1, keepdims=True) + EPS) * gain


def _sigmoid(x):
    return 1.0 / (1.0 + jnp.exp(-x))


def _softplus(x):
    return jnp.maximum(x, 0.0) + jnp.log1p(jnp.exp(-jnp.abs(x)))


def _split3(x):
    x1 = x.astype(BF16)
    r1 = x - x1.astype(F32)
    x2 = r1.astype(BF16)
    x3 = (r1 - x2.astype(F32)).astype(BF16)
    return x1, x2, x3


def _params(sem, vmem=V7X_VMEM_LIMIT):
    return pltpu.CompilerParams(dimension_semantics=sem, vmem_limit_bytes=vmem)


def _norm_small_kernel(x_ref, gain_ref, wc_ref, wr_ref, alog_c_ref, dtb_c_ref, alog_r_ref, dtb_r_ref,
                       h_ref, col_ref, row_ref, *, heads):
    tm = x_ref.shape[0]
    h = _rms(x_ref[...], gain_ref[...]).astype(BF16)
    h_ref[...] = h
    pc = jnp.dot(h, wc_ref[...], preferred_element_type=F32)
    pr = lax.dot_general(wr_ref[...], h, (((1,), (1,)), ((), ())), preferred_element_type=F32)
    beta_c = _sigmoid(pc[:, :heads])
    g_c = -jnp.exp(alog_c_ref[...]) * _softplus(pc[:, heads:] + dtb_c_ref[...])
    beta_r = _sigmoid(pr[:heads, :])
    g_r = -jnp.exp(alog_r_ref[...]) * _softplus(pr[heads:, :] + dtb_r_ref[...])
    ri = lax.broadcasted_iota(jnp.int32, (tm, tm), 0)
    ci = lax.broadcasted_iota(jnp.int32, (tm, tm), 1)
    same = (ri // DN_CHUNK) == (ci // DN_CHUNK)
    ltri = jnp.where(same & (ci <= ri), 1.0, 0.0).astype(BF16)
    utri = jnp.where(same & (ri <= ci), 1.0, 0.0).astype(BF16)
    ones = jnp.where(same, 1.0, 0.0).astype(BF16)

    def left(m, parts):
        return sum(jnp.dot(m, p, preferred_element_type=F32) for p in parts)

    def right(parts, m):
        return sum(jnp.dot(p, m, preferred_element_type=F32) for p in parts)

    cs, rs = _split3(g_c), _split3(g_r)
    gc_c, gl_c = left(ltri, cs), left(ones, cs)
    gc_r, gl_r = right(rs, utri), right(rs, ones)
    col_ref[...] = jnp.concatenate([beta_c, gc_c, jnp.exp(gc_c), jnp.exp(gl_c - gc_c)], axis=1)
    row_ref[...] = jnp.concatenate([beta_r, gc_r, jnp.exp(gl_r)], axis=0)


def _norm_small(x2d, norm_mix, w_ba, a_log, dt_bias, *, tm=512):
    s, d = x2d.shape
    heads = a_log.shape[0]
    wc = w_ba.astype(BF16)
    wr = jnp.transpose(w_ba).astype(BF16)
    full = lambda shape: pl.BlockSpec(shape, lambda i: (0,) * len(shape))
    return pl.pallas_call(
        functools.partial(_norm_small_kernel, heads=heads),
        grid=(s // tm,),
        in_specs=[pl.BlockSpec((tm, d), lambda i: (i, 0)), full((1, d)), full((d, 2 * heads)),
                  full((2 * heads, d)), full((1, heads)), full((1, heads)), full((heads, 1)), full((heads, 1))],
        out_specs=[pl.BlockSpec((tm, d), lambda i: (i, 0)),
                   pl.BlockSpec((tm, 4 * heads), lambda i: (i, 0)), pl.BlockSpec((3 * heads, tm), lambda i: (0, i))],
        out_shape=[jax.ShapeDtypeStruct((s, d), BF16),
                   jax.ShapeDtypeStruct((s, 4 * heads), F32), jax.ShapeDtypeStruct((3 * heads, s), F32)],
        compiler_params=_params(("arbitrary",)),
        name="norm_small",
    )(x2d, norm_mix.reshape(1, d), wc, wr, a_log.reshape(1, heads), dt_bias.reshape(1, heads),
      a_log.reshape(heads, 1), dt_bias.reshape(heads, 1))


def _gmlp_kernel(h_ref, w_ref, vgain_ref, ws_ref, bs_ref, o_ref, *, groups):
    tm = h_ref.shape[0]
    gw = groups * GM_GROUP_DIM
    uv = jnp.dot(h_ref[...], w_ref[...], preferred_element_type=F32)
    uv = 0.5 * uv * (1.0 + lax.erf(uv * (2.0 ** -0.5)))
    u = uv[:, :gw]
    v = _rms(uv[:, gw:], vgain_ref[...]).astype(BF16)
    ri = lax.broadcasted_iota(jnp.int32, (GM_CHUNK, GM_CHUNK), 0)
    ci = lax.broadcasted_iota(jnp.int32, (GM_CHUNK, GM_CHUNK), 1)
    causal = ci <= ri
    for g in range(groups):
        wg = jnp.where(causal, ws_ref[g], 0.0).astype(BF16)
        bias = bs_ref[:, g:g + 1]
        cols = slice(g * GM_GROUP_DIM, (g + 1) * GM_GROUP_DIM)
        for c in range(tm // GM_CHUNK):
            rows = slice(c * GM_CHUNK, (c + 1) * GM_CHUNK)
            sv = jnp.dot(wg, v[rows, cols], preferred_element_type=F32) + bias
            o_ref[rows, cols] = (u[rows, cols] * sv).astype(o_ref.dtype)


def _gmlp(hb, w_uv, v_gain, w_s, b_s, *, tm=256):
    s, d = hb.shape
    groups = w_s.shape[0]
    gw = groups * GM_GROUP_DIM
    return pl.pallas_call(
        functools.partial(_gmlp_kernel, groups=groups),
        grid=(s // tm,),
        in_specs=[pl.BlockSpec((tm, d), lambda i: (i, 0)),
                  pl.BlockSpec((d, 2 * gw), lambda i: (0, 0)),
                  pl.BlockSpec((1, gw), lambda i: (0, 0)),
                  pl.BlockSpec((groups, GM_CHUNK, GM_CHUNK), lambda i: (0, 0, 0)),
                  pl.BlockSpec((GM_CHUNK, groups), lambda i: (0, 0))],
        out_specs=pl.BlockSpec((tm, gw), lambda i: (i, 0)),
        out_shape=jax.ShapeDtypeStruct((s, gw), BF16),
        compiler_params=_params(("arbitrary",)),
        name="gmlp",
    )(hb, w_uv, v_gain.reshape(1, gw), w_s, jnp.transpose(b_s))


def _proj_kernel(h_ref, w_ref, cw_ref, o_ref, tail_ref, cbuf_ref, *, nw, seq_tiles):
    j = pl.program_id(0)
    i = pl.program_id(1)
    tm, tn = o_ref.shape
    dh = DN_HEAD_DIM

    sub = cbuf_ref.shape[1]
    blocks = [slice(b * sub, (b + 1) * sub) for b in range(tn // sub)]

    def pipelined(finish):
        pending = None
        for cols in blocks:
            a = jnp.dot(h_ref[...], w_ref[:, cols], preferred_element_type=F32)
            if pending is not None:
                finish(*pending)
            pending = (cols, a)
        finish(*pending)

    def silu(y):
        hy = 0.5 * y
        return hy + hy * jnp.tanh(hy)

    def conv_silu(cols, a):
        cbuf_ref[0:8, :] = tail_ref[:, cols]
        cbuf_ref[8:8 + tm, :] = a
        w = cw_ref[:, cols]
        y = a * w[DN_CONV - 1:DN_CONV, :]
        for jj in range(DN_CONV - 1):
            off = 8 - (DN_CONV - 1) + jj
            y = y + cbuf_ref[off:off + tm, :] * w[jj:jj + 1, :]
        tail_ref[:, cols] = a[tm - 8:, :]
        return silu(y)

    def l2norm_store(scale):
        def finish(cols, a):
            y = conv_silu(cols, a)
            for h in range(sub // dh):
                yh = y[:, h * dh:(h + 1) * dh]
                inv = lax.rsqrt(jnp.sum(yh * yh, axis=-1, keepdims=True) + EPS) * scale
                o_ref[:, cols.start + h * dh:cols.start + (h + 1) * dh] = (yh * inv).astype(o_ref.dtype)
        return finish

    def store(fn):
        def finish(cols, a):
            o_ref[:, cols] = fn(cols, a).astype(o_ref.dtype)
        return finish

    @pl.when((j < 3 * nw) & (i % seq_tiles == 0))
    def _():
        tail_ref[...] = jnp.zeros_like(tail_ref)

    @pl.when(j < nw)
    def _():
        pipelined(l2norm_store(dh ** -0.5))

    @pl.when((j >= nw) & (j < 2 * nw))
    def _():
        pipelined(l2norm_store(1.0))

    @pl.when((j >= 2 * nw) & (j < 3 * nw))
    def _():
        pipelined(store(conv_silu))

    @pl.when((j >= 3 * nw) & (j < 4 * nw))
    def _():
        pipelined(store(lambda cols, a: silu(a)))

    @pl.when(j >= 4 * nw)
    def _():
        pipelined(store(lambda cols, a: _sigmoid(a)))


def _proj(hb, w_all, conv_w, *, col0, n_cols, dn_w, seq, tm=1024, tn=2048):
    s, d = hb.shape
    j0 = col0 // tn
    nw = dn_w // tn
    n_conv = 3 * nw
    return pl.pallas_call(
        functools.partial(_proj_kernel, nw=nw, seq_tiles=seq // tm),
        grid=(n_cols // tn, s // tm),
        in_specs=[pl.BlockSpec((tm, d), lambda j, i: (i, 0)), pl.BlockSpec((d, tn), lambda j, i: (0, j0 + j)),
                  pl.BlockSpec((DN_CONV, tn), lambda j, i: (0, jnp.minimum(j, n_conv - 1)))],
        out_specs=pl.BlockSpec((tm, tn), lambda j, i: (i, j)),
        out_shape=jax.ShapeDtypeStruct((s, n_cols), BF16),
        scratch_shapes=[pltpu.VMEM((8, tn), F32), pltpu.VMEM((tm + 8, MXU_COLS), F32)],
        compiler_params=_params(("arbitrary", "arbitrary")),
        name="proj",
    )(hb, w_all, conv_w)


def _delta_kernel(q_ref, k_ref, v_ref, z_ref, col_ref, row_ref, onorm_ref, o_ref,
                  state_ref, wq_s, uc_s, qk_s, kb_s, eg_s, et_s, el_s, *, heads, seq_tiles):
    tile = q_ref.shape[0]
    c = DN_CHUNK
    dh = DN_HEAD_DIM
    t = pl.program_id(0)
    wslot = t % 2
    rslot = (t + 1) % 2

    @pl.when(t == 0)
    def _():
        for ref in (wq_s, uc_s, qk_s, kb_s, eg_s, et_s, el_s):
            ref[...] = jnp.zeros_like(ref)

    @pl.when((t == 0) | ((t - 1) % seq_tiles == 0))
    def _():
        state_ref[...] = jnp.zeros_like(state_ref)

    ri = lax.broadcasted_iota(jnp.int32, (c, c), 0)
    ci = lax.broadcasted_iota(jnp.int32, (c, c), 1)
    tril = ci <= ri
    strict = ci < ri
    blk16 = (ri // 16) == (ci // 16)
    blk32 = (ri // 32) == (ci // 32)
    eye = jnp.where(ri == ci, 1.0, 0.0)
    onorm = onorm_ref[...]

    nch = tile // c
    units = [(i, n) for i in range(heads) for n in range(nch)]
    nu = len(units)

    def rows_of(n):
        return slice(n * c, (n + 1) * c)

    def lanes_of(i):
        return slice(i * dh, (i + 1) * dh)

    def recurrence():
        state = [state_ref[i] for i in range(heads)]
        for n in range(nch):
            us = [i * nch + n for i in range(heads)]
            ws_qs = [jnp.dot(wq_s[rslot, u], state[i].astype(BF16), preferred_element_type=F32)
                     for i, u in enumerate(us)]
            yield
            v_new = [uc_s[rslot, u] - ws_qs[i][:c] for i, u in enumerate(us)]
            o_in = [jnp.dot(qk_s[rslot, u], v_new[i].astype(BF16), preferred_element_type=F32)
                    for i, u in enumerate(us)]
            kv = [_mm_tn(kb_s[rslot, u], v_new[i] * et_s[rslot, u]) for i, u in enumerate(us)]
            yield
            state = [state[i] * el_s[rslot, u][0:1, :] + kv[i] for i, u in enumerate(us)]
            for i, u in enumerate(us):
                o = ws_qs[i][c:] * eg_s[rslot, u] + o_in[i]
                out = _rms(o, onorm) * z_ref[rows_of(n), lanes_of(i)].astype(F32)
                o_ref[rows_of(n), lanes_of(i)] = out.astype(o_ref.dtype)
            yield
        for i in range(heads):
            state_ref[i] = state[i]

    rec = recurrence()

    def tick():
        next(rec, None)

    col = col_ref[...]
    lane = lax.broadcasted_iota(jnp.int32, col.shape, 1)

    def col_of(section, i):
        return jnp.sum(jnp.where(lane == section * heads + i, col, 0.0), axis=-1, keepdims=True)

    b_col = [col_of(0, i) for i in range(heads)]
    g_col = [col_of(1, i) for i in range(heads)]
    eg_col = [col_of(2, i) for i in range(heads)]
    et_col = [col_of(3, i) for i in range(heads)]
    b_row = [row_ref[i:i + 1, :] for i in range(heads)]
    g_row = [row_ref[heads + i:heads + i + 1, :] for i in range(heads)]
    el_row = [row_ref[2 * heads + i:2 * heads + i + 1, :] for i in range(heads)]
    tick()

    kb = [k_ref[rows_of(n), lanes_of(i)] for i, n in units]
    qb = [q_ref[rows_of(n), lanes_of(i)] for i, n in units]
    vb = [v_ref[rows_of(n), lanes_of(i)] for i, n in units]
    gc = [g_col[i][rows_of(n)] for i, n in units]
    bc = [b_col[i][rows_of(n)] for i, n in units]
    gr = [g_row[i][:, rows_of(n)] for i, n in units]
    br = [b_row[i][:, rows_of(n)] for i, n in units]
    for u, (i, n) in enumerate(units):
        kb_s[wslot, u] = kb[u]
        wq_s[wslot, u, c:2 * c, :] = qb[u]
        eg_s[wslot, u] = jnp.broadcast_to(eg_col[i][rows_of(n)], (c, dh))
        et_s[wslot, u] = jnp.broadcast_to(et_col[i][rows_of(n)], (c, dh))
        el_s[wslot, u] = jnp.broadcast_to(el_row[i][:, n * c:n * c + 1], (8, dh))
    tick()
    e = [jnp.exp(gc[u] - gr[u]) for u in range(nu)]
    kq = [_mm_nt(jnp.concatenate([kb[u], qb[u]], axis=0), kb[u]) for u in range(nu)]
    tick()
    for u in range(nu):
        qk_s[wslot, u] = jnp.where(tril, kq[u][c:] * e[u], 0.0).astype(BF16)
    nm = [jnp.where(strict, kq[u][:c] * (e[u] * (-bc[u])), 0.0) for u in range(nu)]
    dg = [jnp.where(blk16, nm[u], 0.0) for u in range(nu)]
    x = [eye + dg[u] for u in range(nu)]
    dk = [_mm(dg[u], dg[u]).astype(BF16) for u in range(nu)]
    tick()
    for _ in range(2):
        r = [_mm(jnp.concatenate([x[u].astype(BF16), dk[u]], axis=0), dk[u]) for u in range(nu)]
        x = [x[u] + r[u][:c] for u in range(nu)]
        dk = [r[u][c:].astype(BF16) for u in range(nu)]
        tick()
    x = [x[u] + _mm(x[u], dk[u]) for u in range(nu)]
    tick()
    for level in (0, 1):
        if level == 0:
            off = [jnp.where(blk32 & jnp.logical_not(blk16), nm[u], 0.0) for u in range(nu)]
        else:
            off = [jnp.where(blk32, 0.0, nm[u]) for u in range(nu)]
        y = [_mm(off[u], x[u]) for u in range(nu)]
        tick()
        x = [x[u] + _mm(x[u], y[u]) for u in range(nu)]
        tick()
    for u in range(nu):
        uc_s[wslot, u] = _mm(x[u] * br[u], vb[u])
        wq_s[wslot, u, 0:c, :] = _mm(x[u] * (br[u] * jnp.exp(gr[u])), kb[u]).astype(BF16)
    for _ in rec:
        pass


def _delta(big, col, row, out_norm, *, width, seq, tile=128):
    s = big.shape[0]
    heads = width // DN_HEAD_DIM
    c, dh = DN_CHUNK, DN_HEAD_DIM
    nt = s // tile
    nu = heads * (tile // c)
    cur = lambda sec: pl.BlockSpec((tile, width), lambda t: (jnp.minimum(t, nt - 1), sec))
    prev = lambda sec: pl.BlockSpec((tile, width), lambda t: (jnp.maximum(t - 1, 0), sec))
    return pl.pallas_call(
        functools.partial(_delta_kernel, heads=heads, seq_tiles=seq // tile),
        grid=(nt + 1,),
        in_specs=[cur(0), cur(1), cur(2), prev(3),
                  pl.BlockSpec((tile, col.shape[1]), lambda t: (jnp.minimum(t, nt - 1), 0)),
                  pl.BlockSpec((row.shape[0], tile), lambda t: (0, jnp.minimum(t, nt - 1))),
                  pl.BlockSpec((1, dh), lambda t: (0, 0))],
        out_specs=prev(0),
        out_shape=jax.ShapeDtypeStruct((s, width), BF16),
        scratch_shapes=[pltpu.VMEM((heads, dh, dh), F32),
                        pltpu.VMEM((2, nu, 2 * c, dh), BF16),
                        pltpu.VMEM((2, nu, c, dh), F32),
                        pltpu.VMEM((2, nu, c, c), BF16),
                        pltpu.VMEM((2, nu, c, dh), BF16),
                        pltpu.VMEM((2, nu, c, dh), F32),
                        pltpu.VMEM((2, nu, c, dh), F32),
                        pltpu.VMEM((2, nu, 8, dh), F32)],
        compiler_params=_params(("arbitrary",)),
        name="delta",
    )(big, big, big, big, col, row, out_norm.reshape(1, dh))


def _merge_kernel(ya_ref, wa_ref, yb_ref, wb_ref, ga_ref, gb_ref, o_ref):
    a = jnp.dot(ya_ref[...], wa_ref[...], preferred_element_type=F32)
    b = jnp.dot(yb_ref[...], wb_ref[...], preferred_element_type=F32)
    o_ref[...] = (ga_ref[...].astype(F32) * a + gb_ref[...].astype(F32) * b).astype(o_ref.dtype)


def _merge(ya, wa, yb, wb, big, *, gate_col0, tm=1024, tn=1024):
    s, ka = ya.shape
    kb = yb.shape[1]
    n = wa.shape[1]
    ga0 = gate_col0 // tn
    gb0 = (gate_col0 + n) // tn
    return pl.pallas_call(
        _merge_kernel,
        grid=(s // tm, n // tn),
        in_specs=[pl.BlockSpec((tm, ka), lambda i, j: (i, 0)), pl.BlockSpec((ka, tn), lambda i, j: (0, j)),
                  pl.BlockSpec((tm, kb), lambda i, j: (i, 0)), pl.BlockSpec((kb, tn), lambda i, j: (0, j)),
                  pl.BlockSpec((tm, tn), lambda i, j: (i, ga0 + j)), pl.BlockSpec((tm, tn), lambda i, j: (i, gb0 + j))],
        out_specs=pl.BlockSpec((tm, tn), lambda i, j: (i, j)),
        out_shape=jax.ShapeDtypeStruct((s, n), BF16),
        compiler_params=_params(("arbitrary", "arbitrary")),
        name="merge",
    )(ya, wa, yb, wb, big, big)


def _resid_mm_kernel(a_ref, w_ref, r_ref, o_ref):
    o_ref[...] = r_ref[...] + jnp.dot(a_ref[...], w_ref[...], preferred_element_type=F32)


def _resid_mm(a, w, resid, *, tm, tn, name):
    s, k = a.shape
    n = w.shape[1]
    return pl.pallas_call(
        _resid_mm_kernel,
        grid=(n // tn, s // tm),
        in_specs=[pl.BlockSpec((tm, k), lambda j, i: (i, 0)), pl.BlockSpec((k, tn), lambda j, i: (0, j)),
                  pl.BlockSpec((tm, tn), lambda j, i: (i, j))],
        out_specs=pl.BlockSpec((tm, tn), lambda j, i: (i, j)),
        out_shape=jax.ShapeDtypeStruct((s, n), F32),
        compiler_params=_params(("arbitrary", "arbitrary")),
        name=name,
    )(a, w, resid)


def _ffn_up_kernel(x_ref, gain_ref, wg_ref, wu_ref, o_ref, h_ref):
    @pl.when(pl.program_id(1) == 0)
    def _():
        h_ref[...] = _rms(x_ref[...], gain_ref[...]).astype(BF16)

    h = h_ref[...]
    g = jnp.dot(h, wg_ref[...], preferred_element_type=F32)
    u = jnp.dot(h, wu_ref[...], preferred_element_type=F32)
    o_ref[...] = (g * _sigmoid(g) * u).astype(o_ref.dtype)


def _ffn_up(x1, gain, w_gate_up, *, tm=1024, tn=512):
    s, d = x1.shape
    dff = w_gate_up.shape[1] // 2
    nj = dff // tn
    return pl.pallas_call(
        _ffn_up_kernel,
        grid=(s // tm, nj),
        in_specs=[pl.BlockSpec((tm, d), lambda i, j: (i, 0)), pl.BlockSpec((1, d), lambda i, j: (0, 0)),
                  pl.BlockSpec((d, tn), lambda i, j: (0, j)), pl.BlockSpec((d, tn), lambda i, j: (0, nj + j))],
        out_specs=pl.BlockSpec((tm, tn), lambda i, j: (i, j)),
        out_shape=jax.ShapeDtypeStruct((s, dff), BF16),
        scratch_shapes=[pltpu.VMEM((tm, d), BF16)],
        compiler_params=_params(("arbitrary", "arbitrary")),
        name="ffn_up",
    )(x1, gain.reshape(1, d), w_gate_up, w_gate_up)


def _ple_kernel(x_ref, p_ref, pgain_ref, wg_ref, wp_ref, fgain_ref, o_ref, *, final_norm):
    x = x_ref[...]
    gate = _sigmoid(jnp.dot(_rms(x, pgain_ref[...]).astype(BF16), wg_ref[...], preferred_element_type=F32))
    proj = jnp.dot(p_ref[...].astype(BF16), wp_ref[...], preferred_element_type=F32)
    y = x + gate * proj
    o_ref[...] = _rms(y, fgain_ref[...]) if final_norm else y


def _ple(x2, p2d, ple_norm, w_gate, w_proj, norm_final, *, final_norm, tm=512):
    s, d = x2.shape
    pd = p2d.shape[1]
    return pl.pallas_call(
        functools.partial(_ple_kernel, final_norm=final_norm),
        grid=(s // tm,),
        in_specs=[pl.BlockSpec((tm, d), lambda i: (i, 0)), pl.BlockSpec((tm, pd), lambda i: (i, 0)),
                  pl.BlockSpec((1, d), lambda i: (0, 0)), pl.BlockSpec((d, d), lambda i: (0, 0)),
                  pl.BlockSpec((pd, d), lambda i: (0, 0)), pl.BlockSpec((1, d), lambda i: (0, 0))],
        out_specs=pl.BlockSpec((tm, d), lambda i: (i, 0)),
        out_shape=jax.ShapeDtypeStruct((s, d), F32),
        compiler_params=_params(("arbitrary",)),
        name="ple",
    )(x2, p2d, ple_norm.reshape(1, d), w_gate, w_proj, norm_final.reshape(1, d))


def kernel(x, p, norm_mix, w_in, gm_v_norm, gm_w_s, gm_b_s, dn_conv_w, dn_a_log, dn_dt_bias, dn_out_norm,
           w_branch_a, w_branch_b, w_out, norm_ffn, w_gate_up, w_down, ple_norm, w_ple_gate, w_ple_proj,
           norm_final):
    b, s, d = x.shape
    depth = w_in.shape[0]
    gm_w = w_branch_a.shape[1]
    dn_w = w_branch_b.shape[1]
    heads = dn_a_log.shape[1]
    c_uv, c_qkv, c_z = 0, 2 * gm_w, 2 * gm_w + 3 * dn_w
    c_ba = c_z + dn_w
    c_gate = c_ba + 2 * heads
    xs = x.reshape(b * s, d)
    for i in range(depth):
        wi = w_in[i]
        w_all = jnp.concatenate([wi[:, :c_ba], wi[:, c_gate:]], axis=1).astype(BF16)
        hb16, col, row = _norm_small(xs, norm_mix[i], wi[:, c_ba:c_gate], dn_a_log[i], dn_dt_bias[i])
        y_a = _gmlp(hb16, w_all, gm_v_norm[i], gm_w_s[i], gm_b_s[i])
        big = _proj(hb16, w_all, dn_conv_w[i], col0=c_qkv, n_cols=w_all.shape[1] - c_qkv, dn_w=dn_w, seq=s)
        y_b = _delta(big, col, row, dn_out_norm[i], width=dn_w, seq=s)
        merged = _merge(y_a, w_branch_a[i].astype(BF16), y_b, w_branch_b[i].astype(BF16), big,
                        gate_col0=4 * dn_w)
        x1 = _resid_mm(merged, w_out[i].astype(BF16), xs, tm=1024, tn=1024, name="out_proj")
        f = _ffn_up(x1, norm_ffn[i], w_gate_up[i].astype(BF16))
        x2 = _resid_mm(f, w_down[i].astype(BF16), x1, tm=512, tn=1024, name="ffn_down")
        xs = _ple(x2, p[i].reshape(b * s, -1), ple_norm[i], w_ple_gate[i].astype(BF16),
                  w_ple_proj[i].astype(BF16), norm_final, final_norm=(i == depth - 1))
    return xs.reshape(b, s, d)
```

```python
import functools

import jax
import jax.numpy as jnp
from jax import lax
from jax.experimental import pallas as pl
from jax.experimental.pallas import tpu as pltpu

EPS = 1e-6
GM_CHUNK = 128
GM_GROUP_DIM = 128
DN_HEAD_DIM = 128
DN_CONV = 4
DN_CHUNK = 64
MXU_COLS = 256
V7X_VMEM_LIMIT = 56 * 1024 * 1024

F32 = jnp.float32
BF16 = jnp.bfloat16


def _mm(a, b):
    return jnp.dot(a.astype(BF16), b.astype(BF16), preferred_element_type=F32)


def _mm_nt(a, b):
    return lax.dot_general(a.astype(BF16), b.astype(BF16), (((1,), (1,)), ((), ())),
                           preferred_element_type=F32)


def _mm_tn(a, b):
    return lax.dot_general(a.astype(BF16), b.astype(BF16), (((0,), (0,)), ((), ())),
                           preferred_element_type=F32)


def _rms(xf, gain):
    return xf * lax.rsqrt(jnp.mean(xf * xf, axis=-1, keepdims=True) + EPS) * gain


def _sigmoid(x):
    return 1.0 / (1.0 + jnp.exp(-x))


def _softplus(x):
    return jnp.maximum(x, 0.0) + jnp.log1p(jnp.exp(-jnp.abs(x)))


def _split3(x):
    x1 = x.astype(BF16)
    r1 = x - x1.astype(F32)
    x2 = r1.astype(BF16)
    x3 = (r1 - x2.astype(F32)).astype(BF16)
    return x1, x2, x3


def _params(sem, vmem=V7X_VMEM_LIMIT):
    return pltpu.CompilerParams(dimension_semantics=sem, vmem_limit_bytes=vmem)


def _norm_small_kernel(x_ref, gain_ref, wc_ref, wr_ref, alog_c_ref, dtb_c_ref, alog_r_ref, dtb_r_ref,
                       h_ref, col_ref, row_ref, *, heads):
    tm = x_ref.shape[0]
    h = _rms(x_ref[...], gain_ref[...]).astype(BF16)
    h_ref[...] = h
    pc = jnp.dot(h, wc_ref[...], preferred_element_type=F32)
    pr = lax.dot_general(wr_ref[...], h, (((1,), (1,)), ((), ())), preferred_element_type=F32)
    beta_c = _sigmoid(pc[:, :heads])
    g_c = -jnp.exp(alog_c_ref[...]) * _softplus(pc[:, heads:] + dtb_c_ref[...])
    beta_r = _sigmoid(pr[:heads, :])
    g_r = -jnp.exp(alog_r_ref[...]) * _softplus(pr[heads:, :] + dtb_r_ref[...])
    ri = lax.broadcasted_iota(jnp.int32, (tm, tm), 0)
    ci = lax.broadcasted_iota(jnp.int32, (tm, tm), 1)
    same = (ri // DN_CHUNK) == (ci // DN_CHUNK)
    ltri = jnp.where(same & (ci <= ri), 1.0, 0.0).astype(BF16)
    utri = jnp.where(same & (ri <= ci), 1.0, 0.0).astype(BF16)
    ones = jnp.where(same, 1.0, 0.0).astype(BF16)

    def left(m, parts):
        return sum(jnp.dot(m, p, preferred_element_type=F32) for p in parts)

    def right(parts, m):
        return sum(jnp.dot(p, m, preferred_element_type=F32) for p in parts)

    cs, rs = _split3(g_c), _split3(g_r)
    gc_c, gl_c = left(ltri, cs), left(ones, cs)
    gc_r, gl_r = right(rs, utri), right(rs, ones)
    col_ref[...] = jnp.concatenate([beta_c, gc_c, jnp.exp(gc_c), jnp.exp(gl_c - gc_c)], axis=1)
    row_ref[...] = jnp.concatenate([beta_r, gc_r, jnp.exp(gl_r)], axis=0)


def _norm_small(x2d, norm_mix, w_ba_t, a_log, dt_bias, *, tm=512):
    s, d = x2d.shape
    heads = a_log.shape[0]
    wr = w_ba_t.astype(BF16)
    wc = jnp.transpose(w_ba_t).astype(BF16)
    full = lambda shape: pl.BlockSpec(shape, lambda i: (0,) * len(shape))
    return pl.pallas_call(
        functools.partial(_norm_small_kernel, heads=heads),
        grid=(s // tm,),
        in_specs=[pl.BlockSpec((tm, d), lambda i: (i, 0)), full((1, d)), full((d, 2 * heads)),
                  full((2 * heads, d)), full((1, heads)), full((1, heads)), full((heads, 1)), full((heads, 1))],
        out_specs=[pl.BlockSpec((tm, d), lambda i: (i, 0)),
                   pl.BlockSpec((tm, 4 * heads), lambda i: (i, 0)), pl.BlockSpec((3 * heads, tm), lambda i: (0, i))],
        out_shape=[jax.ShapeDtypeStruct((s, d), BF16),
                   jax.ShapeDtypeStruct((s, 4 * heads), F32), jax.ShapeDtypeStruct((3 * heads, s), F32)],
        compiler_params=_params(("arbitrary",)),
        name="norm_small",
    )(x2d, norm_mix.reshape(1, d), wc, wr, a_log.reshape(1, heads), dt_bias.reshape(1, heads),
      a_log.reshape(heads, 1), dt_bias.reshape(heads, 1))


def _gmlp_kernel(h_ref, w_ref, vgain_ref, ws_ref, bs_ref, o_ref, wb_ref, *, groups):
    tm = h_ref.shape[0]
    gw = groups * GM_GROUP_DIM

    @pl.when(pl.program_id(0) == 0)
    def _():
        wb_ref[...] = w_ref[...].astype(BF16)

    uv = _mm_nt(h_ref[...], wb_ref[...])
    uv = 0.5 * uv * (1.0 + lax.erf(uv * (2.0 ** -0.5)))
    u = uv[:, :gw]
    v = _rms(uv[:, gw:], vgain_ref[...]).astype(BF16)
    ri = lax.broadcasted_iota(jnp.int32, (GM_CHUNK, GM_CHUNK), 0)
    ci = lax.broadcasted_iota(jnp.int32, (GM_CHUNK, GM_CHUNK), 1)
    causal = ci <= ri
    for g in range(groups):
        wg = jnp.where(causal, ws_ref[g], 0.0).astype(BF16)
        bias = bs_ref[:, g:g + 1]
        cols = slice(g * GM_GROUP_DIM, (g + 1) * GM_GROUP_DIM)
        for c in range(tm // GM_CHUNK):
            rows = slice(c * GM_CHUNK, (c + 1) * GM_CHUNK)
            sv = jnp.dot(wg, v[rows, cols], preferred_element_type=F32) + bias
            o_ref[rows, cols] = (u[rows, cols] * sv).astype(o_ref.dtype)


def _gmlp(hb, w_uv, v_gain, w_s, b_s, *, tm=512):
    s, d = hb.shape
    groups = w_s.shape[0]
    gw = groups * GM_GROUP_DIM
    return pl.pallas_call(
        functools.partial(_gmlp_kernel, groups=groups),
        grid=(s // tm,),
        in_specs=[pl.BlockSpec((tm, d), lambda i: (i, 0)),
                  pl.BlockSpec((2 * gw, d), lambda i: (0, 0), pipeline_mode=pl.Buffered(1)),
                  pl.BlockSpec((1, gw), lambda i: (0, 0)),
                  pl.BlockSpec((groups, GM_CHUNK, GM_CHUNK), lambda i: (0, 0, 0)),
                  pl.BlockSpec((GM_CHUNK, groups), lambda i: (0, 0))],
        out_specs=pl.BlockSpec((tm, gw), lambda i: (i, 0)),
        out_shape=jax.ShapeDtypeStruct((s, gw), BF16),
        scratch_shapes=[pltpu.VMEM((2 * gw, d), BF16)],
        compiler_params=_params(("arbitrary",)),
        name="gmlp",
    )(hb, w_uv, v_gain.reshape(1, gw), w_s, jnp.transpose(b_s))


def _proj_kernel(h_ref, w_ref, wx_ref, cw_ref, o_ref, wb_ref, tail_ref, cbuf_ref, *, nw, seq_tiles):
    j = pl.program_id(0)
    i = pl.program_id(1)
    tm, tn = o_ref.shape
    dh = DN_HEAD_DIM
    shift = wx_ref.shape[0]

    @pl.when((i == 0) & (j < 4 * nw))
    def _():
        wb_ref[...] = w_ref[...].astype(BF16)

    @pl.when((i == 0) & (j >= 4 * nw))
    def _():
        wb_ref[0:tn - shift, :] = w_ref[shift:tn, :].astype(BF16)
        wb_ref[tn - shift:tn, :] = wx_ref[...].astype(BF16)

    sub = cbuf_ref.shape[1]
    blocks = [slice(b * sub, (b + 1) * sub) for b in range(tn // sub)]

    def pipelined(finish):
        pending = None
        for cols in blocks:
            a = _mm_nt(h_ref[...], wb_ref[cols, :])
            if pending is not None:
                finish(*pending)
            pending = (cols, a)
        finish(*pending)

    def silu(y):
        hy = 0.5 * y
        return hy + hy * jnp.tanh(hy)

    def conv_silu(cols, a):
        cbuf_ref[0:8, :] = tail_ref[:, cols]
        cbuf_ref[8:8 + tm, :] = a
        w = cw_ref[:, cols]
        y = a * w[DN_CONV - 1:DN_CONV, :]
        for jj in range(DN_CONV - 1):
            off = 8 - (DN_CONV - 1) + jj
            y = y + cbuf_ref[off:off + tm, :] * w[jj:jj + 1, :]
        tail_ref[:, cols] = a[tm - 8:, :]
        return silu(y)

    def l2norm_store(scale):
        def finish(cols, a):
            y = conv_silu(cols, a)
            for h in range(sub // dh):
                yh = y[:, h * dh:(h + 1) * dh]
                inv = lax.rsqrt(jnp.sum(yh * yh, axis=-1, keepdims=True) + EPS) * scale
                o_ref[:, cols.start + h * dh:cols.start + (h + 1) * dh] = (yh * inv).astype(o_ref.dtype)
        return finish

    def store(fn):
        def finish(cols, a):
            o_ref[:, cols] = fn(cols, a).astype(o_ref.dtype)
        return finish

    @pl.when((j < 3 * nw) & (i % seq_tiles == 0))
    def _():
        tail_ref[...] = jnp.zeros_like(tail_ref)

    @pl.when(j < nw)
    def _():
        pipelined(l2norm_store(dh ** -0.5))

    @pl.when((j >= nw) & (j < 2 * nw))
    def _():
        pipelined(l2norm_store(1.0))

    @pl.when((j >= 2 * nw) & (j < 3 * nw))
    def _():
        pipelined(store(conv_silu))

    @pl.when((j >= 3 * nw) & (j < 4 * nw))
    def _():
        pipelined(store(lambda cols, a: silu(a)))

    @pl.when(j >= 4 * nw)
    def _():
        pipelined(store(lambda cols, a: _sigmoid(a)))


def _proj(hb, w_t, conv_w, *, row0, gate_row0, dn_w, seq, tm=512, tn=1024):
    s, d = hb.shape
    nw = dn_w // tn
    n_main = 4 * nw
    n_cols = 4 * dn_w + (w_t.shape[0] - gate_row0)
    g_blk, shift = divmod(gate_row0, tn)
    assert row0 % tn == 0 and shift % 16 == 0 and 0 < shift and tn % shift == 0
    per = tn // shift

    def w_main(j, i):
        return jnp.where(j < n_main, row0 // tn + j, g_blk + j - n_main), 0

    def w_extra(j, i):
        return (g_blk + 1 + jnp.maximum(j - n_main, 0)) * per, 0

    return pl.pallas_call(
        functools.partial(_proj_kernel, nw=nw, seq_tiles=seq // tm),
        grid=(n_cols // tn, s // tm),
        in_specs=[pl.BlockSpec((tm, d), lambda j, i: (i, 0)),
                  pl.BlockSpec((tn, d), w_main),
                  pl.BlockSpec((shift, d), w_extra),
                  pl.BlockSpec((DN_CONV, tn), lambda j, i: (0, jnp.minimum(j, 3 * nw - 1)))],
        out_specs=pl.BlockSpec((tm, tn), lambda j, i: (i, j)),
        out_shape=jax.ShapeDtypeStruct((s, n_cols), BF16),
        scratch_shapes=[pltpu.VMEM((tn, d), BF16), pltpu.VMEM((8, tn), F32), pltpu.VMEM((tm + 8, MXU_COLS), F32)],
        compiler_params=_params(("arbitrary", "arbitrary")),
        name="proj",
    )(hb, w_t, w_t, conv_w)


def _delta_kernel(q_ref, k_ref, v_ref, z_ref, col_ref, row_ref, onorm_ref, o_ref,
                  state_ref, wq_s, uc_s, qk_s, kb_s, eg_s, et_s, el_s, *, heads, seq_tiles):
    tile = q_ref.shape[0]
    c = DN_CHUNK
    dh = DN_HEAD_DIM
    t = pl.program_id(0)
    wslot = t % 2
    rslot = (t + 1) % 2

    @pl.when(t == 0)
    def _():
        for ref in (wq_s, uc_s, qk_s, kb_s, eg_s, et_s, el_s):
            ref[...] = jnp.zeros_like(ref)

    @pl.when((t == 0) | ((t - 1) % seq_tiles == 0))
    def _():
        state_ref[...] = jnp.zeros_like(state_ref)

    ri = lax.broadcasted_iota(jnp.int32, (c, c), 0)
    ci = lax.broadcasted_iota(jnp.int32, (c, c), 1)
    tril = ci <= ri
    strict = ci < ri
    blk16 = (ri // 16) == (ci // 16)
    blk32 = (ri // 32) == (ci // 32)
    eye = jnp.where(ri == ci, 1.0, 0.0)
    onorm = onorm_ref[...]

    nch = tile // c
    units = [(i, n) for i in range(heads) for n in range(nch)]
    nu = len(units)

    def rows_of(n):
        return slice(n * c, (n + 1) * c)

    def lanes_of(i):
        return slice(i * dh, (i + 1) * dh)

    def recurrence():
        state = [state_ref[i] for i in range(heads)]
        for n in range(nch):
            us = [i * nch + n for i in range(heads)]
            ws_qs = [jnp.dot(wq_s[rslot, u], state[i].astype(BF16), preferred_element_type=F32)
                     for i, u in enumerate(us)]
            yield
            v_new = [uc_s[rslot, u] - ws_qs[i][:c] for i, u in enumerate(us)]
            o_in = [jnp.dot(qk_s[rslot, u], v_new[i].astype(BF16), preferred_element_type=F32)
                    for i, u in enumerate(us)]
            kv = [_mm_tn(kb_s[rslot, u], v_new[i] * et_s[rslot, u]) for i, u in enumerate(us)]
            yield
            state = [state[i] * el_s[rslot, u][0:1, :] + kv[i] for i, u in enumerate(us)]
            for i, u in enumerate(us):
                o = ws_qs[i][c:] * eg_s[rslot, u] + o_in[i]
                out = _rms(o, onorm) * z_ref[rows_of(n), lanes_of(i)].astype(F32)
                o_ref[rows_of(n), lanes_of(i)] = out.astype(o_ref.dtype)
            yield
        for i in range(heads):
            state_ref[i] = state[i]

    rec = recurrence()

    def tick():
        next(rec, None)

    col = col_ref[...]
    lane = lax.broadcasted_iota(jnp.int32, col.shape, 1)

    def col_of(section, i):
        return jnp.sum(jnp.where(lane == section * heads + i, col, 0.0), axis=-1, keepdims=True)

    b_col = [col_of(0, i) for i in range(heads)]
    g_col = [col_of(1, i) for i in range(heads)]
    eg_col = [col_of(2, i) for i in range(heads)]
    et_col = [col_of(3, i) for i in range(heads)]
    b_row = [row_ref[i:i + 1, :] for i in range(heads)]
    g_row = [row_ref[heads + i:heads + i + 1, :] for i in range(heads)]
    el_row = [row_ref[2 * heads + i:2 * heads + i + 1, :] for i in range(heads)]
    tick()

    kb = [k_ref[rows_of(n), lanes_of(i)] for i, n in units]
    qb = [q_ref[rows_of(n), lanes_of(i)] for i, n in units]
    vb = [v_ref[rows_of(n), lanes_of(i)] for i, n in units]
    gc = [g_col[i][rows_of(n)] for i, n in units]
    bc = [b_col[i][rows_of(n)] for i, n in units]
    gr = [g_row[i][:, rows_of(n)] for i, n in units]
    br = [b_row[i][:, rows_of(n)] for i, n in units]
    for u, (i, n) in enumerate(units):
        kb_s[wslot, u] = kb[u]
        wq_s[wslot, u, c:2 * c, :] = qb[u]
        eg_s[wslot, u] = jnp.broadcast_to(eg_col[i][rows_of(n)], (c, dh))
        et_s[wslot, u] = jnp.broadcast_to(et_col[i][rows_of(n)], (c, dh))
        el_s[wslot, u] = jnp.broadcast_to(el_row[i][:, n * c:n * c + 1], (8, dh))
    tick()
    e = [jnp.exp(gc[u] - gr[u]) for u in range(nu)]
    kq = [_mm_nt(jnp.concatenate([kb[u], qb[u]], axis=0), kb[u]) for u in range(nu)]
    tick()
    for u in range(nu):
        qk_s[wslot, u] = jnp.where(tril, kq[u][c:] * e[u], 0.0).astype(BF16)
    nm = [jnp.where(strict, kq[u][:c] * (e[u] * (-bc[u])), 0.0) for u in range(nu)]
    dg = [jnp.where(blk16, nm[u], 0.0) for u in range(nu)]
    x = [eye + dg[u] for u in range(nu)]
    dk = [_mm(dg[u], dg[u]).astype(BF16) for u in range(nu)]
    tick()
    for _ in range(2):
        r = [_mm(jnp.concatenate([x[u].astype(BF16), dk[u]], axis=0), dk[u]) for u in range(nu)]
        x = [x[u] + r[u][:c] for u in range(nu)]
        dk = [r[u][c:].astype(BF16) for u in range(nu)]
        tick()
    x = [x[u] + _mm(x[u], dk[u]) for u in range(nu)]
    tick()
    for level in (0, 1):
        if level == 0:
            off = [jnp.where(blk32 & jnp.logical_not(blk16), nm[u], 0.0) for u in range(nu)]
        else:
            off = [jnp.where(blk32, 0.0, nm[u]) for u in range(nu)]
        y = [_mm(off[u], x[u]) for u in range(nu)]
        tick()
        x = [x[u] + _mm(x[u], y[u]) for u in range(nu)]
        tick()
    for u in range(nu):
        uc_s[wslot, u] = _mm(x[u] * br[u], vb[u])
        wq_s[wslot, u, 0:c, :] = _mm(x[u] * (br[u] * jnp.exp(gr[u])), kb[u]).astype(BF16)
    for _ in rec:
        pass


def _delta(big, col, row, out_norm, *, width, seq, tile=128):
    s = big.shape[0]
    heads = width // DN_HEAD_DIM
    c, dh = DN_CHUNK, DN_HEAD_DIM
    nt = s // tile
    nu = heads * (tile // c)
    cur = lambda sec: pl.BlockSpec((tile, width), lambda t: (jnp.minimum(t, nt - 1), sec))
    prev = lambda sec: pl.BlockSpec((tile, width), lambda t: (jnp.maximum(t - 1, 0), sec))
    return pl.pallas_call(
        functools.partial(_delta_kernel, heads=heads, seq_tiles=seq // tile),
        grid=(nt + 1,),
        in_specs=[cur(0), cur(1), cur(2), prev(3),
                  pl.BlockSpec((tile, col.shape[1]), lambda t: (jnp.minimum(t, nt - 1), 0)),
                  pl.BlockSpec((row.shape[0], tile), lambda t: (0, jnp.minimum(t, nt - 1))),
                  pl.BlockSpec((1, dh), lambda t: (0, 0))],
        out_specs=prev(0),
        out_shape=jax.ShapeDtypeStruct((s, width), BF16),
        scratch_shapes=[pltpu.VMEM((heads, dh, dh), F32),
                        pltpu.VMEM((2, nu, 2 * c, dh), BF16),
                        pltpu.VMEM((2, nu, c, dh), F32),
                        pltpu.VMEM((2, nu, c, c), BF16),
                        pltpu.VMEM((2, nu, c, dh), BF16),
                        pltpu.VMEM((2, nu, c, dh), F32),
                        pltpu.VMEM((2, nu, c, dh), F32),
                        pltpu.VMEM((2, nu, 8, dh), F32)],
        compiler_params=_params(("arbitrary",)),
        name="delta",
    )(big, big, big, big, col, row, out_norm.reshape(1, dh))


def _merge_kernel(ya_ref, wa_ref, yb_ref, wb_ref, ga_ref, gb_ref, o_ref):
    a = jnp.dot(ya_ref[...], wa_ref[...], preferred_element_type=F32)
    b = jnp.dot(yb_ref[...], wb_ref[...], preferred_element_type=F32)
    o_ref[...] = (ga_ref[...].astype(F32) * a + gb_ref[...].astype(F32) * b).astype(o_ref.dtype)


def _merge(ya, wa, yb, wb, big, *, gate_col0, tm=1024, tn=1024):
    s, ka = ya.shape
    kb = yb.shape[1]
    n = wa.shape[1]
    ga0 = gate_col0 // tn
    gb0 = (gate_col0 + n) // tn
    return pl.pallas_call(
        _merge_kernel,
        grid=(s // tm, n // tn),
        in_specs=[pl.BlockSpec((tm, ka), lambda i, j: (i, 0)), pl.BlockSpec((ka, tn), lambda i, j: (0, j)),
                  pl.BlockSpec((tm, kb), lambda i, j: (i, 0)), pl.BlockSpec((kb, tn), lambda i, j: (0, j)),
                  pl.BlockSpec((tm, tn), lambda i, j: (i, ga0 + j)), pl.BlockSpec((tm, tn), lambda i, j: (i, gb0 + j))],
        out_specs=pl.BlockSpec((tm, tn), lambda i, j: (i, j)),
        out_shape=jax.ShapeDtypeStruct((s, n), BF16),
        compiler_params=_params(("arbitrary", "arbitrary")),
        name="merge",
    )(ya, wa, yb, wb, big, big)


def _resid_mm_kernel(a_ref, w_ref, r_ref, o_ref):
    o_ref[...] = r_ref[...] + jnp.dot(a_ref[...], w_ref[...], preferred_element_type=F32)


def _resid_mm(a, w, resid, *, tm, tn, name):
    s, k = a.shape
    n = w.shape[1]
    return pl.pallas_call(
        _resid_mm_kernel,
        grid=(n // tn, s // tm),
        in_specs=[pl.BlockSpec((tm, k), lambda j, i: (i, 0)), pl.BlockSpec((k, tn), lambda j, i: (0, j)),
                  pl.BlockSpec((tm, tn), lambda j, i: (i, j))],
        out_specs=pl.BlockSpec((tm, tn), lambda j, i: (i, j)),
        out_shape=jax.ShapeDtypeStruct((s, n), F32),
        compiler_params=_params(("arbitrary", "arbitrary")),
        name=name,
    )(a, w, resid)


def _ffn_up_kernel(x_ref, gain_ref, wg_ref, wu_ref, o_ref, h_ref):
    @pl.when(pl.program_id(1) == 0)
    def _():
        h_ref[...] = _rms(x_ref[...], gain_ref[...]).astype(BF16)

    h = h_ref[...]
    g = jnp.dot(h, wg_ref[...], preferred_element_type=F32)
    u = jnp.dot(h, wu_ref[...], preferred_element_type=F32)
    o_ref[...] = (g * _sigmoid(g) * u).astype(o_ref.dtype)


def _ffn_up(x1, gain, w_gate_up, *, tm=1024, tn=512):
    s, d = x1.shape
    dff = w_gate_up.shape[1] // 2
    nj = dff // tn
    return pl.pallas_call(
        _ffn_up_kernel,
        grid=(s // tm, nj),
        in_specs=[pl.BlockSpec((tm, d), lambda i, j: (i, 0)), pl.BlockSpec((1, d), lambda i, j: (0, 0)),
                  pl.BlockSpec((d, tn), lambda i, j: (0, j)), pl.BlockSpec((d, tn), lambda i, j: (0, nj + j))],
        out_specs=pl.BlockSpec((tm, tn), lambda i, j: (i, j)),
        out_shape=jax.ShapeDtypeStruct((s, dff), BF16),
        scratch_shapes=[pltpu.VMEM((tm, d), BF16)],
        compiler_params=_params(("arbitrary", "arbitrary")),
        name="ffn_up",
    )(x1, gain.reshape(1, d), w_gate_up, w_gate_up)


def _ple_kernel(x_ref, p_ref, pgain_ref, wg_ref, wp_ref, fgain_ref, o_ref, *, final_norm):
    x = x_ref[...]
    gate = _sigmoid(jnp.dot(_rms(x, pgain_ref[...]).astype(BF16), wg_ref[...], preferred_element_type=F32))
    proj = jnp.dot(p_ref[...].astype(BF16), wp_ref[...], preferred_element_type=F32)
    y = x + gate * proj
    o_ref[...] = _rms(y, fgain_ref[...]) if final_norm else y


def _ple(x2, p2d, ple_norm, w_gate, w_proj, norm_final, *, final_norm, tm=512):
    s, d = x2.shape
    pd = p2d.shape[1]
    return pl.pallas_call(
        functools.partial(_ple_kernel, final_norm=final_norm),
        grid=(s // tm,),
        in_specs=[pl.BlockSpec((tm, d), lambda i: (i, 0)), pl.BlockSpec((tm, pd), lambda i: (i, 0)),
                  pl.BlockSpec((1, d), lambda i: (0, 0)), pl.BlockSpec((d, d), lambda i: (0, 0)),
                  pl.BlockSpec((pd, d), lambda i: (0, 0)), pl.BlockSpec((1, d), lambda i: (0, 0))],
        out_specs=pl.BlockSpec((tm, d), lambda i: (i, 0)),
        out_shape=jax.ShapeDtypeStruct((s, d), F32),
        compiler_params=_params(("arbitrary",)),
        name="ple",
    )(x2, p2d, ple_norm.reshape(1, d), w_gate, w_proj, norm_final.reshape(1, d))


def kernel(x, p, norm_mix, w_in, gm_v_norm, gm_w_s, gm_b_s, dn_conv_w, dn_a_log, dn_dt_bias, dn_out_norm,
           w_branch_a, w_branch_b, w_out, norm_ffn, w_gate_up, w_down, ple_norm, w_ple_gate, w_ple_proj,
           norm_final):
    b, s, d = x.shape
    depth = w_in.shape[0]
    gm_w = w_branch_a.shape[1]
    dn_w = w_branch_b.shape[1]
    heads = dn_a_log.shape[1]
    c_uv, c_qkv, c_z = 0, 2 * gm_w, 2 * gm_w + 3 * dn_w
    c_ba = c_z + dn_w
    c_gate = c_ba + 2 * heads
    xs = x.reshape(b * s, d)

    def layer(w, i):
        return w.reshape(w.shape[1:]) if depth == 1 else w[i]

    def layer_bf16(w, i):
        return layer(w.astype(BF16), i)

    for i in range(depth):
        w_t = layer(jnp.swapaxes(w_in, 1, 2), i)
        hb16, col, row = _norm_small(xs, norm_mix[i], w_t[c_ba:c_gate], dn_a_log[i], dn_dt_bias[i])
        y_a = _gmlp(hb16, w_t, gm_v_norm[i], gm_w_s[i], gm_b_s[i])
        big = _proj(hb16, w_t, dn_conv_w[i], row0=c_qkv, gate_row0=c_gate, dn_w=dn_w, seq=s)
        y_b = _delta(big, col, row, dn_out_norm[i], width=dn_w, seq=s)
        merged = _merge(y_a, layer_bf16(w_branch_a, i), y_b, layer_bf16(w_branch_b, i), big, gate_col0=4 * dn_w)
        x1 = _resid_mm(merged, layer_bf16(w_out, i), xs, tm=1024, tn=1024, name="out_proj")
        f = _ffn_up(x1, norm_ffn[i], layer_bf16(w_gate_up, i))
        x2 = _resid_mm(f, layer_bf16(w_down, i), x1, tm=512, tn=1024, name="ffn_down")
        xs = _ple(x2, p[i].reshape(b * s, -1), ple_norm[i], layer_bf16(w_ple_gate, i),
                  layer_bf16(w_ple_proj, i), norm_final, final_norm=(i == depth - 1))
    return xs.reshape(b, s, d)
```

```python
import functools

import jax
import jax.numpy as jnp
from jax import lax
from jax.experimental import pallas as pl
from jax.experimental.pallas import tpu as pltpu

EPS = 1e-6
GM_CHUNK = 128
GM_GROUP_DIM = 128
DN_HEAD_DIM = 128
DN_CONV = 4
DN_CHUNK = 64
MXU_COLS = 256
V7X_VMEM_LIMIT = 56 * 1024 * 1024

F32 = jnp.float32
BF16 = jnp.bfloat16


def _mm(a, b):
    return jnp.dot(a.astype(BF16), b.astype(BF16), preferred_element_type=F32)


def _mm_nt(a, b):
    return lax.dot_general(a.astype(BF16), b.astype(BF16), (((1,), (1,)), ((), ())),
                           preferred_element_type=F32)


def _mm_tn(a, b):
    return lax.dot_general(a.astype(BF16), b.astype(BF16), (((0,), (0,)), ((), ())),
                           preferred_element_type=F32)


def _rms(xf, gain):
    return xf * lax.rsqrt(jnp.mean(xf * xf, axis=-1, keepdims=True) + EPS) * gain


def _sigmoid(x):
    return 1.0 / (1.0 + jnp.exp(-x))


def _softplus(x):
    return jnp.maximum(x, 0.0) + jnp.log1p(jnp.exp(-jnp.abs(x)))


def _split3(x):
    x1 = x.astype(BF16)
    r1 = x - x1.astype(F32)
    x2 = r1.astype(BF16)
    x3 = (r1 - x2.astype(F32)).astype(BF16)
    return x1, x2, x3


def _params(sem, vmem=V7X_VMEM_LIMIT):
    return pltpu.CompilerParams(dimension_semantics=sem, vmem_limit_bytes=vmem)


def _norm_small_kernel(x_ref, gain_ref, wc_ref, wr_ref, alog_c_ref, dtb_c_ref, alog_r_ref, dtb_r_ref,
                       h_ref, col_ref, row_ref, *, heads):
    tm = x_ref.shape[0]
    h = _rms(x_ref[...], gain_ref[...]).astype(BF16)
    h_ref[...] = h
    pc = jnp.dot(h, wc_ref[...], preferred_element_type=F32)
    pr = lax.dot_general(wr_ref[...], h, (((1,), (1,)), ((), ())), preferred_element_type=F32)
    beta_c = _sigmoid(pc[:, :heads])
    g_c = -jnp.exp(alog_c_ref[...]) * _softplus(pc[:, heads:] + dtb_c_ref[...])
    beta_r = _sigmoid(pr[:heads, :])
    g_r = -jnp.exp(alog_r_ref[...]) * _softplus(pr[heads:, :] + dtb_r_ref[...])
    ri = lax.broadcasted_iota(jnp.int32, (tm, tm), 0)
    ci = lax.broadcasted_iota(jnp.int32, (tm, tm), 1)
    same = (ri // DN_CHUNK) == (ci // DN_CHUNK)
    ltri = jnp.where(same & (ci <= ri), 1.0, 0.0).astype(BF16)
    utri = jnp.where(same & (ri <= ci), 1.0, 0.0).astype(BF16)
    ones = jnp.where(same, 1.0, 0.0).astype(BF16)

    def left(m, parts):
        return sum(jnp.dot(m, p, preferred_element_type=F32) for p in parts)

    def right(parts, m):
        return sum(jnp.dot(p, m, preferred_element_type=F32) for p in parts)

    cs, rs = _split3(g_c), _split3(g_r)
    gc_c, gl_c = left(ltri, cs), left(ones, cs)
    gc_r, gl_r = right(rs, utri), right(rs, ones)
    col_ref[...] = jnp.concatenate([beta_c, gc_c, jnp.exp(gc_c), jnp.exp(gl_c - gc_c)], axis=1)
    row_ref[...] = jnp.concatenate([beta_r, gc_r, jnp.exp(gl_r)], axis=0)


def _norm_small(x2d, norm_mix, w_ba_t, a_log, dt_bias, *, tm=512):
    s, d = x2d.shape
    heads = a_log.shape[0]
    wr = w_ba_t.astype(BF16)
    wc = jnp.transpose(w_ba_t).astype(BF16)
    full = lambda shape: pl.BlockSpec(shape, lambda i: (0,) * len(shape))
    return pl.pallas_call(
        functools.partial(_norm_small_kernel, heads=heads),
        grid=(s // tm,),
        in_specs=[pl.BlockSpec((tm, d), lambda i: (i, 0)), full((1, d)), full((d, 2 * heads)),
                  full((2 * heads, d)), full((1, heads)), full((1, heads)), full((heads, 1)), full((heads, 1))],
        out_specs=[pl.BlockSpec((tm, d), lambda i: (i, 0)),
                   pl.BlockSpec((tm, 4 * heads), lambda i: (i, 0)), pl.BlockSpec((3 * heads, tm), lambda i: (0, i))],
        out_shape=[jax.ShapeDtypeStruct((s, d), BF16),
                   jax.ShapeDtypeStruct((s, 4 * heads), F32), jax.ShapeDtypeStruct((3 * heads, s), F32)],
        compiler_params=_params(("arbitrary",)),
        name="norm_small",
    )(x2d, norm_mix.reshape(1, d), wc, wr, a_log.reshape(1, heads), dt_bias.reshape(1, heads),
      a_log.reshape(heads, 1), dt_bias.reshape(heads, 1))


def _gmlp_kernel(h_ref, w_ref, vgain_ref, ws_ref, bs_ref, o_ref, wb_ref, *, groups):
    tm = h_ref.shape[0]
    gw = groups * GM_GROUP_DIM

    @pl.when(pl.program_id(0) == 0)
    def _():
        wb_ref[...] = w_ref[...].astype(BF16)

    uv = _mm_nt(h_ref[...], wb_ref[...])
    uv = 0.5 * uv * (1.0 + lax.erf(uv * (2.0 ** -0.5)))
    u = uv[:, :gw]
    v = _rms(uv[:, gw:], vgain_ref[...]).astype(BF16)
    ri = lax.broadcasted_iota(jnp.int32, (GM_CHUNK, GM_CHUNK), 0)
    ci = lax.broadcasted_iota(jnp.int32, (GM_CHUNK, GM_CHUNK), 1)
    causal = ci <= ri
    for g in range(groups):
        wg = jnp.where(causal, ws_ref[g], 0.0).astype(BF16)
        bias = bs_ref[:, g:g + 1]
        cols = slice(g * GM_GROUP_DIM, (g + 1) * GM_GROUP_DIM)
        for c in range(tm // GM_CHUNK):
            rows = slice(c * GM_CHUNK, (c + 1) * GM_CHUNK)
            sv = jnp.dot(wg, v[rows, cols], preferred_element_type=F32) + bias
            o_ref[rows, cols] = (u[rows, cols] * sv).astype(o_ref.dtype)


def _gmlp(hb, w_uv, v_gain, w_s, b_s, *, tm=512):
    s, d = hb.shape
    groups = w_s.shape[0]
    gw = groups * GM_GROUP_DIM
    return pl.pallas_call(
        functools.partial(_gmlp_kernel, groups=groups),
        grid=(s // tm,),
        in_specs=[pl.BlockSpec((tm, d), lambda i: (i, 0)),
                  pl.BlockSpec((2 * gw, d), lambda i: (0, 0), pipeline_mode=pl.Buffered(1)),
                  pl.BlockSpec((1, gw), lambda i: (0, 0)),
                  pl.BlockSpec((groups, GM_CHUNK, GM_CHUNK), lambda i: (0, 0, 0)),
                  pl.BlockSpec((GM_CHUNK, groups), lambda i: (0, 0))],
        out_specs=pl.BlockSpec((tm, gw), lambda i: (i, 0)),
        out_shape=jax.ShapeDtypeStruct((s, gw), BF16),
        scratch_shapes=[pltpu.VMEM((2 * gw, d), BF16)],
        compiler_params=_params(("arbitrary",)),
        name="gmlp",
    )(hb, w_uv, v_gain.reshape(1, gw), w_s, jnp.transpose(b_s))


def _proj_kernel(h_ref, w_ref, wx_ref, cw_ref, o_ref, wb_ref, tail_ref, cbuf_ref, *, nw, seq_tiles):
    j = pl.program_id(0)
    i = pl.program_id(1)
    tm, tn = o_ref.shape
    dh = DN_HEAD_DIM
    shift = wx_ref.shape[0]

    @pl.when((i == 0) & (j < 4 * nw))
    def _():
        wb_ref[...] = w_ref[...].astype(BF16)

    @pl.when((i == 0) & (j >= 4 * nw))
    def _():
        wb_ref[0:tn - shift, :] = w_ref[shift:tn, :].astype(BF16)
        wb_ref[tn - shift:tn, :] = wx_ref[...].astype(BF16)

    sub = cbuf_ref.shape[1]
    blocks = [slice(b * sub, (b + 1) * sub) for b in range(tn // sub)]

    def pipelined(finish):
        pending = None
        for cols in blocks:
            a = _mm_nt(h_ref[...], wb_ref[cols, :])
            if pending is not None:
                finish(*pending)
            pending = (cols, a)
        finish(*pending)

    def silu(y):
        hy = 0.5 * y
        return hy + hy * jnp.tanh(hy)

    def conv_silu(cols, a):
        cbuf_ref[0:8, :] = tail_ref[:, cols]
        cbuf_ref[8:8 + tm, :] = a
        w = cw_ref[:, cols]
        y = a * w[DN_CONV - 1:DN_CONV, :]
        for jj in range(DN_CONV - 1):
            off = 8 - (DN_CONV - 1) + jj
            y = y + cbuf_ref[off:off + tm, :] * w[jj:jj + 1, :]
        tail_ref[:, cols] = a[tm - 8:, :]
        return silu(y)

    def l2norm_store(scale):
        def finish(cols, a):
            y = conv_silu(cols, a)
            for h in range(sub // dh):
                yh = y[:, h * dh:(h + 1) * dh]
                inv = lax.rsqrt(jnp.sum(yh * yh, axis=-1, keepdims=True) + EPS) * scale
                o_ref[:, cols.start + h * dh:cols.start + (h + 1) * dh] = (yh * inv).astype(o_ref.dtype)
        return finish

    def store(fn):
        def finish(cols, a):
            o_ref[:, cols] = fn(cols, a).astype(o_ref.dtype)
        return finish

    @pl.when((j < 3 * nw) & (i % seq_tiles == 0))
    def _():
        tail_ref[...] = jnp.zeros_like(tail_ref)

    @pl.when(j < nw)
    def _():
        pipelined(l2norm_store(dh ** -0.5))

    @pl.when((j >= nw) & (j < 2 * nw))
    def _():
        pipelined(l2norm_store(1.0))

    @pl.when((j >= 2 * nw) & (j < 3 * nw))
    def _():
        pipelined(store(conv_silu))

    @pl.when((j >= 3 * nw) & (j < 4 * nw))
    def _():
        pipelined(store(lambda cols, a: silu(a)))

    @pl.when(j >= 4 * nw)
    def _():
        pipelined(store(lambda cols, a: _sigmoid(a)))


def _proj(hb, w_t, conv_w, *, row0, gate_row0, dn_w, seq, tm=512, tn=1024):
    s, d = hb.shape
    nw = dn_w // tn
    n_main = 4 * nw
    n_cols = 4 * dn_w + (w_t.shape[0] - gate_row0)
    g_blk, shift = divmod(gate_row0, tn)
    assert row0 % tn == 0 and shift % 16 == 0 and 0 < shift and tn % shift == 0
    per = tn // shift

    def w_main(j, i):
        return jnp.where(j < n_main, row0 // tn + j, g_blk + j - n_main), 0

    def w_extra(j, i):
        return (g_blk + 1 + jnp.maximum(j - n_main, 0)) * per, 0

    return pl.pallas_call(
        functools.partial(_proj_kernel, nw=nw, seq_tiles=seq // tm),
        grid=(n_cols // tn, s // tm),
        in_specs=[pl.BlockSpec((tm, d), lambda j, i: (i, 0)),
                  pl.BlockSpec((tn, d), w_main),
                  pl.BlockSpec((shift, d), w_extra),
                  pl.BlockSpec((DN_CONV, tn), lambda j, i: (0, jnp.minimum(j, 3 * nw - 1)))],
        out_specs=pl.BlockSpec((tm, tn), lambda j, i: (i, j)),
        out_shape=jax.ShapeDtypeStruct((s, n_cols), BF16),
        scratch_shapes=[pltpu.VMEM((tn, d), BF16), pltpu.VMEM((8, tn), F32), pltpu.VMEM((tm + 8, MXU_COLS), F32)],
        compiler_params=_params(("arbitrary", "arbitrary")),
        name="proj",
    )(hb, w_t, w_t, conv_w)


def _delta_kernel(q_ref, k_ref, v_ref, z_ref, col_ref, row_ref, onorm_ref, o_ref,
                  state_ref, wq_s, uc_s, qk_s, kb_s, eg_s, et_s, el_s, *, heads, seq_tiles, group, stride):
    tile = q_ref.shape[0]
    c = DN_CHUNK
    dh = DN_HEAD_DIM
    t = pl.program_id(0)
    wslot = t % 2
    rslot = (t + 1) % 2

    @pl.when(t == 0)
    def _():
        for ref in (wq_s, uc_s, qk_s, kb_s, eg_s, et_s, el_s):
            ref[...] = jnp.zeros_like(ref)

    @pl.when((t == 0) | ((t - 1) % seq_tiles == 0))
    def _():
        state_ref[...] = jnp.zeros_like(state_ref)

    nch = tile // c
    assert nch == 2
    ri = lax.broadcasted_iota(jnp.int32, (c, 2 * c), 0)
    li = lax.broadcasted_iota(jnp.int32, (c, 2 * c), 1)
    ci = li % c
    left_half = li < c
    tril = ci <= ri
    strict = ci < ri
    blk16 = (ri // 16) == (ci // 16)
    blk32 = (ri // 32) == (ci // 32)
    eye = jnp.where(ri == ci, 1.0, 0.0)
    onorm = onorm_ref[...]
    zero_b = jnp.zeros((c, dh), BF16)

    def rows_of(n):
        return slice(n * c, (n + 1) * c)

    def lanes_of(i):
        return slice(i * dh, (i + 1) * dh)

    def recurrence():
        state = [state_ref[i] for i in range(heads)]
        for n in range(nch):
            us = [i * nch + n for i in range(heads)]
            ws_qs = [jnp.dot(wq_s[rslot, u], state[i].astype(BF16), preferred_element_type=F32)
                     for i, u in enumerate(us)]
            yield
            v_new = [uc_s[rslot, u] - ws_qs[i][:c] for i, u in enumerate(us)]
            pad = (lambda v: jnp.concatenate([v, zero_b], axis=0)) if n == 0 else \
                  (lambda v: jnp.concatenate([zero_b, v], axis=0))
            o_in = [jnp.dot(qk_s[rslot, i], pad(v_new[i].astype(BF16)), preferred_element_type=F32)
                    for i in range(heads)]
            kv = [_mm_tn(kb_s[rslot, u], v_new[i] * et_s[rslot, u]) for i, u in enumerate(us)]
            yield
            state = [state[i] * el_s[rslot, u][0:1, :] + kv[i] for i, u in enumerate(us)]
            for i, u in enumerate(us):
                o = ws_qs[i][c:] * eg_s[rslot, u] + o_in[i]
                out = _rms(o, onorm) * z_ref[rows_of(n), lanes_of(i)].astype(F32)
                o_ref[rows_of(n), lanes_of(i)] = out.astype(o_ref.dtype)
            yield
        for i in range(heads):
            state_ref[i] = state[i]

    rec = recurrence()

    def tick():
        next(rec, None)

    col = col_ref[...]
    lane = lax.broadcasted_iota(jnp.int32, col.shape, 1)

    def col_of(section, i):
        return jnp.sum(jnp.where(lane == section * heads + i, col, 0.0), axis=-1, keepdims=True)

    b_col = [col_of(0, i) for i in range(heads)]
    g_col = [col_of(1, i) for i in range(heads)]
    eg_col = [col_of(2, i) for i in range(heads)]
    et_col = [col_of(3, i) for i in range(heads)]
    b_row = [row_ref[i:i + 1, :] for i in range(heads)]
    g_row = [row_ref[heads + i:heads + i + 1, :] for i in range(heads)]
    el_row = [row_ref[2 * heads + i:2 * heads + i + 1, :] for i in range(heads)]
    tick()

    chunk = lambda ref, i, n: ref[rows_of(n), lanes_of(i)]

    def block_diag(a, b):
        return jnp.concatenate([jnp.concatenate([a, zero_b], axis=1), jnp.concatenate([zero_b, b], axis=1)], axis=0)

    def packed_diag(y):
        yb = y.astype(BF16)
        zero = jnp.zeros_like(yb)
        return jnp.concatenate([jnp.where(left_half, yb, zero), jnp.where(left_half, zero, yb)], axis=0)

    def col_packed(cols, i):
        return jnp.where(left_half, cols[i][rows_of(0)], cols[i][rows_of(1)])

    def prep(hs):
        k0, k1 = {i: chunk(k_ref, i, 0) for i in hs}, {i: chunk(k_ref, i, 1) for i in hs}
        q0, q1 = {i: chunk(q_ref, i, 0) for i in hs}, {i: chunk(q_ref, i, 1) for i in hs}
        for i in hs:
            for n, (kk, qq) in enumerate(((k0[i], q0[i]), (k1[i], q1[i]))):
                u = i * nch + n
                kb_s[wslot, u] = kk
                wq_s[wslot, u, c:2 * c, :] = qq
                eg_s[wslot, u] = jnp.broadcast_to(eg_col[i][rows_of(n)], (c, dh))
                et_s[wslot, u] = jnp.broadcast_to(et_col[i][rows_of(n)], (c, dh))
                el_s[wslot, u] = jnp.broadcast_to(el_row[i][:, n * c:n * c + 1], (8, dh))
        yield
        e = {i: jnp.exp(col_packed(g_col, i) - g_row[i]) for i in hs}
        kd = {i: block_diag(k0[i], k1[i]) for i in hs}
        kq = {i: _mm_nt(jnp.concatenate([jnp.concatenate([k0[i], k1[i]], axis=1),
                                         jnp.concatenate([q0[i], q1[i]], axis=1)], axis=0), kd[i]) for i in hs}
        yield
        for i in hs:
            qk_s[wslot, i] = jnp.where(tril, kq[i][c:] * e[i], 0.0).astype(BF16)
        nm = {i: jnp.where(strict, kq[i][:c] * (e[i] * (-col_packed(b_col, i))), 0.0) for i in hs}
        dg = {i: jnp.where(blk16, nm[i], 0.0) for i in hs}
        x = {i: eye + dg[i] for i in hs}
        dk = {i: _mm(dg[i], packed_diag(dg[i])).astype(BF16) for i in hs}
        yield
        for _ in range(2):
            r = {i: _mm(jnp.concatenate([x[i].astype(BF16), dk[i]], axis=0), packed_diag(dk[i])) for i in hs}
            x = {i: x[i] + r[i][:c] for i in hs}
            dk = {i: r[i][c:].astype(BF16) for i in hs}
            yield
        x = {i: x[i] + _mm(x[i], packed_diag(dk[i])) for i in hs}
        yield
        for level in (0, 1):
            if level == 0:
                off = {i: jnp.where(blk32 & jnp.logical_not(blk16), nm[i], 0.0) for i in hs}
            else:
                off = {i: jnp.where(blk32, 0.0, nm[i]) for i in hs}
            y = {i: _mm(off[i], packed_diag(x[i])) for i in hs}
            yield
            x = {i: x[i] + _mm(x[i], packed_diag(y[i])) for i in hs}
            yield
        for i in hs:
            b_r, g_r = b_row[i], g_row[i]
            vd = block_diag(chunk(v_ref, i, 0), chunk(v_ref, i, 1))
            uc = _mm(x[i] * b_r, vd)
            wc = _mm(x[i] * (b_r * jnp.exp(g_r)), kd[i])
            for n in range(nch):
                uc_s[wslot, i * nch + n] = uc[:, n * dh:(n + 1) * dh]
                wq_s[wslot, i * nch + n, 0:c, :] = wc[:, n * dh:(n + 1) * dh].astype(BF16)
        yield

    stage = 0
    for g0 in range(0, heads, group):
        for _ in prep(range(g0, min(g0 + group, heads))):
            stage += 1
            if stage % stride == 0:
                tick()
    for _ in rec:
        pass


def _delta(big, col, row, out_norm, *, width, seq, tile=128, group=16, stride=1):
    s = big.shape[0]
    heads = width // DN_HEAD_DIM
    c, dh = DN_CHUNK, DN_HEAD_DIM
    nt = s // tile
    nu = heads * (tile // c)
    cur = lambda sec: pl.BlockSpec((tile, width), lambda t: (jnp.minimum(t, nt - 1), sec))
    prev = lambda sec: pl.BlockSpec((tile, width), lambda t: (jnp.maximum(t - 1, 0), sec))
    return pl.pallas_call(
        functools.partial(_delta_kernel, heads=heads, seq_tiles=seq // tile, group=group, stride=stride),
        grid=(nt + 1,),
        in_specs=[cur(0), cur(1), cur(2), prev(3),
                  pl.BlockSpec((tile, col.shape[1]), lambda t: (jnp.minimum(t, nt - 1), 0)),
                  pl.BlockSpec((row.shape[0], tile), lambda t: (0, jnp.minimum(t, nt - 1))),
                  pl.BlockSpec((1, dh), lambda t: (0, 0))],
        out_specs=prev(0),
        out_shape=jax.ShapeDtypeStruct((s, width), BF16),
        scratch_shapes=[pltpu.VMEM((heads, dh, dh), F32),
                        pltpu.VMEM((2, nu, 2 * c, dh), BF16),
                        pltpu.VMEM((2, nu, c, dh), F32),
                        pltpu.VMEM((2, heads, c, 2 * c), BF16),
                        pltpu.VMEM((2, nu, c, dh), BF16),
                        pltpu.VMEM((2, nu, c, dh), F32),
                        pltpu.VMEM((2, nu, c, dh), F32),
                        pltpu.VMEM((2, nu, 8, dh), F32)],
        compiler_params=_params(("arbitrary",)),
        name="delta",
    )(big, big, big, big, col, row, out_norm.reshape(1, dh))


def _merge_kernel(ya_ref, wa_ref, yb_ref, wb_ref, ga_ref, gb_ref, o_ref):
    a = jnp.dot(ya_ref[...], wa_ref[...], preferred_element_type=F32)
    b = jnp.dot(yb_ref[...], wb_ref[...], preferred_element_type=F32)
    o_ref[...] = (ga_ref[...].astype(F32) * a + gb_ref[...].astype(F32) * b).astype(o_ref.dtype)


def _merge(ya, wa, yb, wb, big, *, gate_col0, tm=1024, tn=1024):
    s, ka = ya.shape
    kb = yb.shape[1]
    n = wa.shape[1]
    ga0 = gate_col0 // tn
    gb0 = (gate_col0 + n) // tn
    return pl.pallas_call(
        _merge_kernel,
        grid=(s // tm, n // tn),
        in_specs=[pl.BlockSpec((tm, ka), lambda i, j: (i, 0)), pl.BlockSpec((ka, tn), lambda i, j: (0, j)),
                  pl.BlockSpec((tm, kb), lambda i, j: (i, 0)), pl.BlockSpec((kb, tn), lambda i, j: (0, j)),
                  pl.BlockSpec((tm, tn), lambda i, j: (i, ga0 + j)), pl.BlockSpec((tm, tn), lambda i, j: (i, gb0 + j))],
        out_specs=pl.BlockSpec((tm, tn), lambda i, j: (i, j)),
        out_shape=jax.ShapeDtypeStruct((s, n), BF16),
        compiler_params=_params(("arbitrary", "arbitrary")),
        name="merge",
    )(ya, wa, yb, wb, big, big)


def _resid_mm_kernel(a_ref, w_ref, r_ref, o_ref):
    o_ref[...] = r_ref[...] + jnp.dot(a_ref[...], w_ref[...], preferred_element_type=F32)


def _resid_mm(a, w, resid, *, tm, tn, name):
    s, k = a.shape
    n = w.shape[1]
    return pl.pallas_call(
        _resid_mm_kernel,
        grid=(n // tn, s // tm),
        in_specs=[pl.BlockSpec((tm, k), lambda j, i: (i, 0)), pl.BlockSpec((k, tn), lambda j, i: (0, j)),
                  pl.BlockSpec((tm, tn), lambda j, i: (i, j))],
        out_specs=pl.BlockSpec((tm, tn), lambda j, i: (i, j)),
        out_shape=jax.ShapeDtypeStruct((s, n), F32),
        compiler_params=_params(("arbitrary", "arbitrary")),
        name=name,
    )(a, w, resid)


def _ffn_up_kernel(x_ref, gain_ref, wg_ref, wu_ref, o_ref, h_ref):
    @pl.when(pl.program_id(1) == 0)
    def _():
        h_ref[...] = _rms(x_ref[...], gain_ref[...]).astype(BF16)

    def finish(cols, g, u):
        hg = 0.5 * g
        o_ref[:, cols] = ((hg + hg * jnp.tanh(hg)) * u).astype(o_ref.dtype)

    pending = None
    for b in range(o_ref.shape[1] // MXU_COLS):
        cols = slice(b * MXU_COLS, (b + 1) * MXU_COLS)
        g = jnp.dot(h_ref[...], wg_ref[:, cols], preferred_element_type=F32)
        u = jnp.dot(h_ref[...], wu_ref[:, cols], preferred_element_type=F32)
        if pending is not None:
            finish(*pending)
        pending = (cols, g, u)
    finish(*pending)


def _ffn_up(x1, gain, w_gate_up, *, tm=1024, tn=512):
    s, d = x1.shape
    dff = w_gate_up.shape[1] // 2
    nj = dff // tn
    return pl.pallas_call(
        _ffn_up_kernel,
        grid=(s // tm, nj),
        in_specs=[pl.BlockSpec((tm, d), lambda i, j: (i, 0)), pl.BlockSpec((1, d), lambda i, j: (0, 0)),
                  pl.BlockSpec((d, tn), lambda i, j: (0, j)), pl.BlockSpec((d, tn), lambda i, j: (0, nj + j))],
        out_specs=pl.BlockSpec((tm, tn), lambda i, j: (i, j)),
        out_shape=jax.ShapeDtypeStruct((s, dff), BF16),
        scratch_shapes=[pltpu.VMEM((tm, d), BF16)],
        compiler_params=_params(("arbitrary", "arbitrary")),
        name="ffn_up",
    )(x1, gain.reshape(1, d), w_gate_up, w_gate_up)


def _ple_kernel(x_ref, p_ref, pgain_ref, wg_ref, wp_ref, fgain_ref, o_ref, *, final_norm):
    x = x_ref[...]
    gate = _sigmoid(jnp.dot(_rms(x, pgain_ref[...]).astype(BF16), wg_ref[...], preferred_element_type=F32))
    proj = jnp.dot(p_ref[...].astype(BF16), wp_ref[...], preferred_element_type=F32)
    y = x + gate * proj
    o_ref[...] = _rms(y, fgain_ref[...]) if final_norm else y


def _ple(x2, p2d, ple_norm, w_gate, w_proj, norm_final, *, final_norm, tm=512):
    s, d = x2.shape
    pd = p2d.shape[1]
    return pl.pallas_call(
        functools.partial(_ple_kernel, final_norm=final_norm),
        grid=(s // tm,),
        in_specs=[pl.BlockSpec((tm, d), lambda i: (i, 0)), pl.BlockSpec((tm, pd), lambda i: (i, 0)),
                  pl.BlockSpec((1, d), lambda i: (0, 0)), pl.BlockSpec((d, d), lambda i: (0, 0)),
                  pl.BlockSpec((pd, d), lambda i: (0, 0)), pl.BlockSpec((1, d), lambda i: (0, 0))],
        out_specs=pl.BlockSpec((tm, d), lambda i: (i, 0)),
        out_shape=jax.ShapeDtypeStruct((s, d), F32),
        compiler_params=_params(("arbitrary",)),
        name="ple",
    )(x2, p2d, ple_norm.reshape(1, d), w_gate, w_proj, norm_final.reshape(1, d))


def kernel(x, p, norm_mix, w_in, gm_v_norm, gm_w_s, gm_b_s, dn_conv_w, dn_a_log, dn_dt_bias, dn_out_norm,
           w_branch_a, w_branch_b, w_out, norm_ffn, w_gate_up, w_down, ple_norm, w_ple_gate, w_ple_proj,
           norm_final):
    b, s, d = x.shape
    depth = w_in.shape[0]
    gm_w = w_branch_a.shape[1]
    dn_w = w_branch_b.shape[1]
    heads = dn_a_log.shape[1]
    c_uv, c_qkv, c_z = 0, 2 * gm_w, 2 * gm_w + 3 * dn_w
    c_ba = c_z + dn_w
    c_gate = c_ba + 2 * heads
    xs = x.reshape(b * s, d)

    def layer(w, i):
        return w.reshape(w.shape[1:]) if depth == 1 else w[i]

    def layer_bf16(w, i):
        return layer(w.astype(BF16), i)

    for i in range(depth):
        w_t = layer(jnp.swapaxes(w_in, 1, 2), i)
        hb16, col, row = _norm_small(xs, norm_mix[i], w_t[c_ba:c_gate], dn_a_log[i], dn_dt_bias[i])
        y_a = _gmlp(hb16, w_t, gm_v_norm[i], gm_w_s[i], gm_b_s[i])
        big = _proj(hb16, w_t, dn_conv_w[i], row0=c_qkv, gate_row0=c_gate, dn_w=dn_w, seq=s)
        y_b = _delta(big, col, row, dn_out_norm[i], width=dn_w, seq=s)
        merged = _merge(y_a, layer_bf16(w_branch_a, i), y_b, layer_bf16(w_branch_b, i), big, gate_col0=4 * dn_w)
        x1 = _resid_mm(merged, layer_bf16(w_out, i), xs, tm=1024, tn=1024, name="out_proj")
        f = _ffn_up(x1, norm_ffn[i], layer_bf16(w_gate_up, i))
        x2 = _resid_mm(f, layer_bf16(w_down, i), x1, tm=512, tn=1024, name="ffn_down")
        xs = _ple(x2, p[i].reshape(b * s, -1), ple_norm[i], layer_bf16(w_ple_gate, i),
                  layer_bf16(w_ple_proj, i), norm_final, final_norm=(i == depth - 1))
    return xs.reshape(b, s, d)
```

```python
import functools

import jax
import jax.numpy as jnp
from jax import lax
from jax.experimental import pallas as pl
from jax.experimental.pallas import tpu as pltpu

EPS = 1e-6
GM_CHUNK = 128
GM_GROUP_DIM = 128
DN_HEAD_DIM = 128
DN_CONV = 4
DN_CHUNK = 64
MXU_COLS = 256
V7X_VMEM_LIMIT = 56 * 1024 * 1024

F32 = jnp.float32
BF16 = jnp.bfloat16


def _mm(a, b):
    return jnp.dot(a.astype(BF16), b.astype(BF16), preferred_element_type=F32)


def _mm_nt(a, b):
    return lax.dot_general(a.astype(BF16), b.astype(BF16), (((1,), (1,)), ((), ())),
                           preferred_element_type=F32)


def _mm_tn(a, b):
    return lax.dot_general(a.astype(BF16), b.astype(BF16), (((0,), (0,)), ((), ())),
                           preferred_element_type=F32)


def _rms(xf, gain):
    return xf * lax.rsqrt(jnp.mean(xf * xf, axis=-1, keepdims=True) + EPS) * gain


def _sigmoid(x):
    return 1.0 / (1.0 + jnp.exp(-x))


def _softplus(x):
    return jnp.maximum(x, 0.0) + jnp.log1p(jnp.exp(-jnp.abs(x)))


def _split3(x):
    x1 = x.astype(BF16)
    r1 = x - x1.astype(F32)
    x2 = r1.astype(BF16)
    x3 = (r1 - x2.astype(F32)).astype(BF16)
    return x1, x2, x3


def _params(sem, vmem=V7X_VMEM_LIMIT):
    return pltpu.CompilerParams(dimension_semantics=sem, vmem_limit_bytes=vmem)


def _norm_small_kernel(x_ref, gain_ref, wc_ref, wr_ref, alog_c_ref, dtb_c_ref, alog_r_ref, dtb_r_ref,
                       h_ref, col_ref, row_ref, *, heads):
    tm = x_ref.shape[0]
    h = _rms(x_ref[...], gain_ref[...]).astype(BF16)
    h_ref[...] = h
    pc = jnp.dot(h, wc_ref[...], preferred_element_type=F32)
    pr = lax.dot_general(wr_ref[...], h, (((1,), (1,)), ((), ())), preferred_element_type=F32)
    beta_c = _sigmoid(pc[:, :heads])
    g_c = -jnp.exp(alog_c_ref[...]) * _softplus(pc[:, heads:] + dtb_c_ref[...])
    beta_r = _sigmoid(pr[:heads, :])
    g_r = -jnp.exp(alog_r_ref[...]) * _softplus(pr[heads:, :] + dtb_r_ref[...])
    ri = lax.broadcasted_iota(jnp.int32, (tm, tm), 0)
    ci = lax.broadcasted_iota(jnp.int32, (tm, tm), 1)
    same = (ri // DN_CHUNK) == (ci // DN_CHUNK)
    ltri = jnp.where(same & (ci <= ri), 1.0, 0.0).astype(BF16)
    utri = jnp.where(same & (ri <= ci), 1.0, 0.0).astype(BF16)
    ones = jnp.where(same, 1.0, 0.0).astype(BF16)

    def left(m, parts):
        return sum(jnp.dot(m, p, preferred_element_type=F32) for p in parts)

    def right(parts, m):
        return sum(jnp.dot(p, m, preferred_element_type=F32) for p in parts)

    cs, rs = _split3(g_c), _split3(g_r)
    gc_c, gl_c = left(ltri, cs), left(ones, cs)
    gc_r, gl_r = right(rs, utri), right(rs, ones)
    col_ref[...] = jnp.concatenate([beta_c, gc_c, jnp.exp(gc_c), jnp.exp(gl_c - gc_c)], axis=1)
    row_ref[...] = jnp.concatenate([beta_r, gc_r, jnp.exp(gl_r)], axis=0)


def _norm_small(x2d, norm_mix, w_ba_t, a_log, dt_bias, *, tm=512):
    s, d = x2d.shape
    heads = a_log.shape[0]
    wr = w_ba_t.astype(BF16)
    wc = jnp.transpose(w_ba_t).astype(BF16)
    full = lambda shape: pl.BlockSpec(shape, lambda i: (0,) * len(shape))
    return pl.pallas_call(
        functools.partial(_norm_small_kernel, heads=heads),
        grid=(s // tm,),
        in_specs=[pl.BlockSpec((tm, d), lambda i: (i, 0)), full((1, d)), full((d, 2 * heads)),
                  full((2 * heads, d)), full((1, heads)), full((1, heads)), full((heads, 1)), full((heads, 1))],
        out_specs=[pl.BlockSpec((tm, d), lambda i: (i, 0)),
                   pl.BlockSpec((tm, 4 * heads), lambda i: (i, 0)), pl.BlockSpec((3 * heads, tm), lambda i: (0, i))],
        out_shape=[jax.ShapeDtypeStruct((s, d), BF16),
                   jax.ShapeDtypeStruct((s, 4 * heads), F32), jax.ShapeDtypeStruct((3 * heads, s), F32)],
        compiler_params=_params(("arbitrary",)),
        name="norm_small",
    )(x2d, norm_mix.reshape(1, d), wc, wr, a_log.reshape(1, heads), dt_bias.reshape(1, heads),
      a_log.reshape(heads, 1), dt_bias.reshape(heads, 1))


def _gmlp_kernel(h_ref, w_ref, vgain_ref, ws_ref, bs_ref, o_ref, wb_ref, *, groups):
    tm = h_ref.shape[0]
    gw = groups * GM_GROUP_DIM

    @pl.when(pl.program_id(0) == 0)
    def _():
        wb_ref[...] = w_ref[...].astype(BF16)

    uv = _mm_nt(h_ref[...], wb_ref[...])
    uv = 0.5 * uv * (1.0 + lax.erf(uv * (2.0 ** -0.5)))
    u = uv[:, :gw]
    v = _rms(uv[:, gw:], vgain_ref[...]).astype(BF16)
    ri = lax.broadcasted_iota(jnp.int32, (GM_CHUNK, GM_CHUNK), 0)
    ci = lax.broadcasted_iota(jnp.int32, (GM_CHUNK, GM_CHUNK), 1)
    causal = ci <= ri
    for g in range(groups):
        wg = jnp.where(causal, ws_ref[g], 0.0).astype(BF16)
        bias = bs_ref[:, g:g + 1]
        cols = slice(g * GM_GROUP_DIM, (g + 1) * GM_GROUP_DIM)
        for c in range(tm // GM_CHUNK):
            rows = slice(c * GM_CHUNK, (c + 1) * GM_CHUNK)
            sv = jnp.dot(wg, v[rows, cols], preferred_element_type=F32) + bias
            o_ref[rows, cols] = (u[rows, cols] * sv).astype(o_ref.dtype)


def _gmlp(hb, w_uv, v_gain, w_s, b_s, *, tm=512):
    s, d = hb.shape
    groups = w_s.shape[0]
    gw = groups * GM_GROUP_DIM
    return pl.pallas_call(
        functools.partial(_gmlp_kernel, groups=groups),
        grid=(s // tm,),
        in_specs=[pl.BlockSpec((tm, d), lambda i: (i, 0)),
                  pl.BlockSpec((2 * gw, d), lambda i: (0, 0), pipeline_mode=pl.Buffered(1)),
                  pl.BlockSpec((1, gw), lambda i: (0, 0)),
                  pl.BlockSpec((groups, GM_CHUNK, GM_CHUNK), lambda i: (0, 0, 0)),
                  pl.BlockSpec((GM_CHUNK, groups), lambda i: (0, 0))],
        out_specs=pl.BlockSpec((tm, gw), lambda i: (i, 0)),
        out_shape=jax.ShapeDtypeStruct((s, gw), BF16),
        scratch_shapes=[pltpu.VMEM((2 * gw, d), BF16)],
        compiler_params=_params(("arbitrary",)),
        name="gmlp",
    )(hb, w_uv, v_gain.reshape(1, gw), w_s, jnp.transpose(b_s))


def _proj_kernel(h_ref, w_ref, wx_ref, cw_ref, mode_ref, o_ref, wb_ref, tail_ref, cbuf_ref, *, nw, seq_tiles):
    j = pl.program_id(0)
    i = pl.program_id(1)
    tm, tn = o_ref.shape
    dh = DN_HEAD_DIM
    shift = wx_ref.shape[0]

    @pl.when((i == 0) & (j < 4 * nw))
    def _():
        wb_ref[...] = w_ref[...].astype(BF16)

    @pl.when((i == 0) & (j >= 4 * nw))
    def _():
        wb_ref[0:tn - shift, :] = w_ref[shift:tn, :].astype(BF16)
        wb_ref[tn - shift:tn, :] = wx_ref[...].astype(BF16)

    sub = cbuf_ref.shape[1]
    blocks = [slice(b * sub, (b + 1) * sub) for b in range(tn // sub)]

    def pipelined(finish):
        pending = None
        for cols in blocks:
            a = _mm_nt(h_ref[...], wb_ref[cols, :])
            if pending is not None:
                finish(*pending)
            pending = (cols, a)
        finish(*pending)

    def silu(y):
        hy = 0.5 * y
        return hy + hy * jnp.tanh(hy)

    def conv_silu(cols, a):
        cbuf_ref[0:8, :] = tail_ref[:, cols]
        cbuf_ref[8:8 + tm, :] = a
        w = cw_ref[:, cols]
        y = a * w[DN_CONV - 1:DN_CONV, :]
        for jj in range(DN_CONV - 1):
            off = 8 - (DN_CONV - 1) + jj
            y = y + cbuf_ref[off:off + tm, :] * w[jj:jj + 1, :]
        tail_ref[:, cols] = a[tm - 8:, :]
        return silu(y)

    silu_w = mode_ref[0, 0:1, 0:1]

    def qk_finish(cols, a):
        y = conv_silu(cols, a)
        for h in range(sub // dh):
            yh = y[:, h * dh:(h + 1) * dh]
            inv = lax.rsqrt(jnp.sum(yh * yh, axis=-1, keepdims=True) + EPS)
            o_ref[:, cols.start + h * dh:cols.start + (h + 1) * dh] = (yh * inv).astype(o_ref.dtype)

    def v_finish(cols, a):
        o_ref[:, cols] = conv_silu(cols, a).astype(o_ref.dtype)

    def gate_finish(cols, a):
        sig = 0.5 + 0.5 * jnp.tanh(0.5 * a)
        o_ref[:, cols] = (sig * (silu_w * a + (1.0 - silu_w))).astype(o_ref.dtype)

    @pl.when((j < 3 * nw) & (i % seq_tiles == 0))
    def _():
        tail_ref[...] = jnp.zeros_like(tail_ref)

    @pl.when(j < 2 * nw)
    def _():
        pipelined(qk_finish)

    @pl.when((j >= 2 * nw) & (j < 3 * nw))
    def _():
        pipelined(v_finish)

    @pl.when(j >= 3 * nw)
    def _():
        pipelined(gate_finish)


def _proj(hb, w_t, conv_w, *, row0, gate_row0, dn_w, seq, tm=1024, tn=1024):
    s, d = hb.shape
    nw = dn_w // tn
    n_main = 4 * nw
    n_cols = 4 * dn_w + (w_t.shape[0] - gate_row0)
    n_tiles = n_cols // tn
    sw = [0.0] * (3 * nw) + [1.0] * nw + [0.0] * (n_tiles - 4 * nw)
    mode = jnp.broadcast_to(jnp.asarray(sw, F32)[:, None, None], (n_tiles, 8, 128))
    g_blk, shift = divmod(gate_row0, tn)
    assert row0 % tn == 0 and shift % 16 == 0 and 0 < shift and tn % shift == 0
    per = tn // shift

    def w_main(j, i):
        return jnp.where(j < n_main, row0 // tn + j, g_blk + j - n_main), 0

    def w_extra(j, i):
        return (g_blk + 1 + jnp.maximum(j - n_main, 0)) * per, 0

    return pl.pallas_call(
        functools.partial(_proj_kernel, nw=nw, seq_tiles=seq // tm),
        grid=(n_cols // tn, s // tm),
        in_specs=[pl.BlockSpec((tm, d), lambda j, i: (i, 0)),
                  pl.BlockSpec((tn, d), w_main),
                  pl.BlockSpec((shift, d), w_extra),
                  pl.BlockSpec((DN_CONV, tn), lambda j, i: (0, jnp.minimum(j, 3 * nw - 1))),
                  pl.BlockSpec((1, 8, 128), lambda j, i: (j, 0, 0))],
        out_specs=pl.BlockSpec((tm, tn), lambda j, i: (i, j)),
        out_shape=jax.ShapeDtypeStruct((s, n_cols), BF16),
        scratch_shapes=[pltpu.VMEM((tn, d), BF16), pltpu.VMEM((8, tn), F32), pltpu.VMEM((tm + 8, MXU_COLS), F32)],
        compiler_params=_params(("arbitrary", "arbitrary")),
        name="proj",
    )(hb, w_t, w_t, conv_w, mode)


def _delta_kernel(q_ref, k_ref, v_ref, z_ref, col_ref, row_ref, onorm_ref, o_ref,
                  state_ref, wq_s, uc_s, qk_s, kb_s, eg_s, et_s, el_s, *, heads, seq_tiles, group, stride):
    tile = q_ref.shape[0]
    c = DN_CHUNK
    dh = DN_HEAD_DIM
    t = pl.program_id(0)
    wslot = t % 2
    rslot = (t + 1) % 2

    @pl.when(t == 0)
    def _():
        for ref in (wq_s, uc_s, qk_s, kb_s, eg_s, et_s, el_s):
            ref[...] = jnp.zeros_like(ref)

    @pl.when((t == 0) | ((t - 1) % seq_tiles == 0))
    def _():
        state_ref[...] = jnp.zeros_like(state_ref)

    nch = tile // c
    assert nch == 2
    ri = lax.broadcasted_iota(jnp.int32, (c, 2 * c), 0)
    li = lax.broadcasted_iota(jnp.int32, (c, 2 * c), 1)
    ci = li % c
    left_half = li < c
    tril = ci <= ri
    strict = ci < ri
    blk16 = (ri // 16) == (ci // 16)
    blk32 = (ri // 32) == (ci // 32)
    eye = jnp.where(ri == ci, 1.0, 0.0)
    onorm = onorm_ref[...]
    zero_b = jnp.zeros((c, dh), BF16)
    q_scale = dh ** -0.5

    def rows_of(n):
        return slice(n * c, (n + 1) * c)

    def lanes_of(i):
        return slice(i * dh, (i + 1) * dh)

    def recurrence():
        state = [state_ref[i] for i in range(heads)]
        for n in range(nch):
            us = [i * nch + n for i in range(heads)]
            ws_qs = [jnp.dot(wq_s[rslot, u], state[i].astype(BF16), preferred_element_type=F32)
                     for i, u in enumerate(us)]
            yield
            v_new = [uc_s[rslot, u] - ws_qs[i][:c] for i, u in enumerate(us)]
            pad = (lambda v: jnp.concatenate([v, zero_b], axis=0)) if n == 0 else \
                  (lambda v: jnp.concatenate([zero_b, v], axis=0))
            o_in = [jnp.dot(qk_s[rslot, i], pad(v_new[i].astype(BF16)), preferred_element_type=F32)
                    for i in range(heads)]
            kv = [_mm_tn(kb_s[rslot, u], v_new[i] * et_s[rslot, u]) for i, u in enumerate(us)]
            yield
            state = [state[i] * el_s[rslot, u][0:1, :] + kv[i] for i, u in enumerate(us)]
            for i, u in enumerate(us):
                o = ws_qs[i][c:] * eg_s[rslot, u] + o_in[i]
                out = _rms(o, onorm) * z_ref[rows_of(n), lanes_of(i)].astype(F32)
                o_ref[rows_of(n), lanes_of(i)] = out.astype(o_ref.dtype)
            yield
        for i in range(heads):
            state_ref[i] = state[i]

    rec = recurrence()

    def tick():
        next(rec, None)

    col = col_ref[...]
    lane = lax.broadcasted_iota(jnp.int32, col.shape, 1)

    def col_of(section, i):
        return jnp.sum(jnp.where(lane == section * heads + i, col, 0.0), axis=-1, keepdims=True)

    b_col = [col_of(0, i) for i in range(heads)]
    g_col = [col_of(1, i) for i in range(heads)]
    eg_col = [col_of(2, i) for i in range(heads)]
    et_col = [col_of(3, i) for i in range(heads)]
    b_row = [row_ref[i:i + 1, :] for i in range(heads)]
    g_row = [row_ref[heads + i:heads + i + 1, :] for i in range(heads)]
    el_row = [row_ref[2 * heads + i:2 * heads + i + 1, :] for i in range(heads)]
    tick()

    chunk = lambda ref, i, n: ref[rows_of(n), lanes_of(i)]

    def block_diag(a, b):
        return jnp.concatenate([jnp.concatenate([a, zero_b], axis=1), jnp.concatenate([zero_b, b], axis=1)], axis=0)

    def packed_diag(y):
        yb = y.astype(BF16)
        zero = jnp.zeros_like(yb)
        return jnp.concatenate([jnp.where(left_half, yb, zero), jnp.where(left_half, zero, yb)], axis=0)

    def col_packed(cols, i):
        return jnp.where(left_half, cols[i][rows_of(0)], cols[i][rows_of(1)])

    def prep(hs):
        k0, k1 = {i: chunk(k_ref, i, 0) for i in hs}, {i: chunk(k_ref, i, 1) for i in hs}
        q0, q1 = {i: chunk(q_ref, i, 0) for i in hs}, {i: chunk(q_ref, i, 1) for i in hs}
        for i in hs:
            for n, (kk, qq) in enumerate(((k0[i], q0[i]), (k1[i], q1[i]))):
                u = i * nch + n
                kb_s[wslot, u] = kk
                wq_s[wslot, u, c:2 * c, :] = qq
                eg_s[wslot, u] = jnp.broadcast_to(eg_col[i][rows_of(n)] * q_scale, (c, dh))
                et_s[wslot, u] = jnp.broadcast_to(et_col[i][rows_of(n)], (c, dh))
                el_s[wslot, u] = jnp.broadcast_to(el_row[i][:, n * c:n * c + 1], (8, dh))
        yield
        e = {i: jnp.exp(col_packed(g_col, i) - g_row[i]) for i in hs}
        kd = {i: block_diag(k0[i], k1[i]) for i in hs}
        kq = {i: _mm_nt(jnp.concatenate([jnp.concatenate([k0[i], k1[i]], axis=1),
                                         jnp.concatenate([q0[i], q1[i]], axis=1)], axis=0), kd[i]) for i in hs}
        yield
        for i in hs:
            qk_s[wslot, i] = jnp.where(tril, kq[i][c:] * (e[i] * q_scale), 0.0).astype(BF16)
        nm = {i: jnp.where(strict, kq[i][:c] * (e[i] * (-col_packed(b_col, i))), 0.0) for i in hs}
        dg = {i: jnp.where(blk16, nm[i], 0.0) for i in hs}
        x = {i: eye + dg[i] for i in hs}
        dk = {i: _mm(dg[i], packed_diag(dg[i])).astype(BF16) for i in hs}
        yield
        for _ in range(2):
            r = {i: _mm(jnp.concatenate([x[i].astype(BF16), dk[i]], axis=0), packed_diag(dk[i])) for i in hs}
            x = {i: x[i] + r[i][:c] for i in hs}
            dk = {i: r[i][c:].astype(BF16) for i in hs}
            yield
        x = {i: x[i] + _mm(x[i], packed_diag(dk[i])) for i in hs}
        yield
        for level in (0, 1):
            if level == 0:
                off = {i: jnp.where(blk32 & jnp.logical_not(blk16), nm[i], 0.0) for i in hs}
            else:
                off = {i: jnp.where(blk32, 0.0, nm[i]) for i in hs}
            y = {i: _mm(off[i], packed_diag(x[i])) for i in hs}
            yield
            x = {i: x[i] + _mm(x[i], packed_diag(y[i])) for i in hs}
            yield
        for i in hs:
            b_r, g_r = b_row[i], g_row[i]
            vd = block_diag(chunk(v_ref, i, 0), chunk(v_ref, i, 1))
            uc = _mm(x[i] * b_r, vd)
            wc = _mm(x[i] * (b_r * jnp.exp(g_r)), kd[i])
            for n in range(nch):
                uc_s[wslot, i * nch + n] = uc[:, n * dh:(n + 1) * dh]
                wq_s[wslot, i * nch + n, 0:c, :] = wc[:, n * dh:(n + 1) * dh].astype(BF16)
        yield

    stage = 0
    for g0 in range(0, heads, group):
        for _ in prep(range(g0, min(g0 + group, heads))):
            stage += 1
            if stage % stride == 0:
                tick()
    for _ in rec:
        pass


def _delta(big, col, row, out_norm, *, width, seq, tile=128, group=16, stride=1):
    s = big.shape[0]
    heads = width // DN_HEAD_DIM
    c, dh = DN_CHUNK, DN_HEAD_DIM
    nt = s // tile
    nu = heads * (tile // c)
    cur = lambda sec: pl.BlockSpec((tile, width), lambda t: (jnp.minimum(t, nt - 1), sec))
    prev = lambda sec: pl.BlockSpec((tile, width), lambda t: (jnp.maximum(t - 1, 0), sec))
    return pl.pallas_call(
        functools.partial(_delta_kernel, heads=heads, seq_tiles=seq // tile, group=group, stride=stride),
        grid=(nt + 1,),
        in_specs=[cur(0), cur(1), cur(2), prev(3),
                  pl.BlockSpec((tile, col.shape[1]), lambda t: (jnp.minimum(t, nt - 1), 0)),
                  pl.BlockSpec((row.shape[0], tile), lambda t: (0, jnp.minimum(t, nt - 1))),
                  pl.BlockSpec((1, dh), lambda t: (0, 0))],
        out_specs=prev(0),
        out_shape=jax.ShapeDtypeStruct((s, width), BF16),
        scratch_shapes=[pltpu.VMEM((heads, dh, dh), F32),
                        pltpu.VMEM((2, nu, 2 * c, dh), BF16),
                        pltpu.VMEM((2, nu, c, dh), F32),
                        pltpu.VMEM((2, heads, c, 2 * c), BF16),
                        pltpu.VMEM((2, nu, c, dh), BF16),
                        pltpu.VMEM((2, nu, c, dh), F32),
                        pltpu.VMEM((2, nu, c, dh), F32),
                        pltpu.VMEM((2, nu, 8, dh), F32)],
        compiler_params=_params(("arbitrary",)),
        name="delta",
    )(big, big, big, big, col, row, out_norm.reshape(1, dh))


def _merge_kernel(ya_ref, wa_ref, yb_ref, wb_ref, ga_ref, gb_ref, o_ref):
    a = jnp.dot(ya_ref[...], wa_ref[...], preferred_element_type=F32)
    b = jnp.dot(yb_ref[...], wb_ref[...], preferred_element_type=F32)
    o_ref[...] = (ga_ref[...].astype(F32) * a + gb_ref[...].astype(F32) * b).astype(o_ref.dtype)


def _merge(ya, wa, yb, wb, big, *, gate_col0, tm=1024, tn=1024):
    s, ka = ya.shape
    kb = yb.shape[1]
    n = wa.shape[1]
    ga0 = gate_col0 // tn
    gb0 = (gate_col0 + n) // tn
    return pl.pallas_call(
        _merge_kernel,
        grid=(s // tm, n // tn),
        in_specs=[pl.BlockSpec((tm, ka), lambda i, j: (i, 0)), pl.BlockSpec((ka, tn), lambda i, j: (0, j)),
                  pl.BlockSpec((tm, kb), lambda i, j: (i, 0)), pl.BlockSpec((kb, tn), lambda i, j: (0, j)),
                  pl.BlockSpec((tm, tn), lambda i, j: (i, ga0 + j)), pl.BlockSpec((tm, tn), lambda i, j: (i, gb0 + j))],
        out_specs=pl.BlockSpec((tm, tn), lambda i, j: (i, j)),
        out_shape=jax.ShapeDtypeStruct((s, n), BF16),
        compiler_params=_params(("arbitrary", "arbitrary")),
        name="merge",
    )(ya, wa, yb, wb, big, big)


def _resid_mm_kernel(a_ref, w_ref, r_ref, o_ref, *wb_ref):
    if wb_ref:
        @pl.when(pl.program_id(1) == 0)
        def _():
            wb_ref[0][...] = w_ref[...].astype(BF16)
        w = wb_ref[0][...]
    else:
        w = w_ref[...]
    o_ref[...] = r_ref[...] + jnp.dot(a_ref[...], w, preferred_element_type=F32)


def _resid_mm(a, w, resid, *, tm, tn, name):
    s, k = a.shape
    n = w.shape[1]
    return pl.pallas_call(
        _resid_mm_kernel,
        grid=(n // tn, s // tm),
        in_specs=[pl.BlockSpec((tm, k), lambda j, i: (i, 0)), pl.BlockSpec((k, tn), lambda j, i: (0, j)),
                  pl.BlockSpec((tm, tn), lambda j, i: (i, j))],
        out_specs=pl.BlockSpec((tm, tn), lambda j, i: (i, j)),
        out_shape=jax.ShapeDtypeStruct((s, n), F32),
        scratch_shapes=[] if w.dtype == BF16 else [pltpu.VMEM((k, tn), BF16)],
        compiler_params=_params(("arbitrary", "arbitrary")),
        name=name,
    )(a, w, resid)


def _ffn_up_kernel(x_ref, gain_ref, wg_ref, wu_ref, o_ref, h_ref):
    @pl.when(pl.program_id(1) == 0)
    def _():
        h_ref[...] = _rms(x_ref[...], gain_ref[...]).astype(BF16)

    def finish(cols, g, u):
        hg = 0.5 * g
        o_ref[:, cols] = ((hg + hg * jnp.tanh(hg)) * u).astype(o_ref.dtype)

    pending = None
    for b in range(o_ref.shape[1] // MXU_COLS):
        cols = slice(b * MXU_COLS, (b + 1) * MXU_COLS)
        g = jnp.dot(h_ref[...], wg_ref[:, cols], preferred_element_type=F32)
        u = jnp.dot(h_ref[...], wu_ref[:, cols], preferred_element_type=F32)
        if pending is not None:
            finish(*pending)
        pending = (cols, g, u)
    finish(*pending)


def _ffn_up(x1, gain, w_gate_up, *, tm=1024, tn=512):
    s, d = x1.shape
    dff = w_gate_up.shape[1] // 2
    nj = dff // tn
    return pl.pallas_call(
        _ffn_up_kernel,
        grid=(s // tm, nj),
        in_specs=[pl.BlockSpec((tm, d), lambda i, j: (i, 0)), pl.BlockSpec((1, d), lambda i, j: (0, 0)),
                  pl.BlockSpec((d, tn), lambda i, j: (0, j)), pl.BlockSpec((d, tn), lambda i, j: (0, nj + j))],
        out_specs=pl.BlockSpec((tm, tn), lambda i, j: (i, j)),
        out_shape=jax.ShapeDtypeStruct((s, dff), BF16),
        scratch_shapes=[pltpu.VMEM((tm, d), BF16)],
        compiler_params=_params(("arbitrary", "arbitrary")),
        name="ffn_up",
    )(x1, gain.reshape(1, d), w_gate_up, w_gate_up)


def _ple_kernel(x_ref, p_ref, pgain_ref, wg_ref, wp_ref, fgain_ref, o_ref, wgb_ref, wpb_ref, *, final_norm):
    @pl.when(pl.program_id(0) == 0)
    def _():
        wgb_ref[...] = wg_ref[...].astype(BF16)
        wpb_ref[...] = wp_ref[...].astype(BF16)

    x = x_ref[...]
    gate = _sigmoid(jnp.dot(_rms(x, pgain_ref[...]).astype(BF16), wgb_ref[...], preferred_element_type=F32))
    proj = jnp.dot(p_ref[...].astype(BF16), wpb_ref[...], preferred_element_type=F32)
    y = x + gate * proj
    o_ref[...] = _rms(y, fgain_ref[...]) if final_norm else y


def _ple(x2, p2d, ple_norm, w_gate, w_proj, norm_final, *, final_norm, tm=512):
    s, d = x2.shape
    pd = p2d.shape[1]
    resident = lambda shape: pl.BlockSpec(shape, lambda i: (0, 0), pipeline_mode=pl.Buffered(1))
    return pl.pallas_call(
        functools.partial(_ple_kernel, final_norm=final_norm),
        grid=(s // tm,),
        in_specs=[pl.BlockSpec((tm, d), lambda i: (i, 0)), pl.BlockSpec((tm, pd), lambda i: (i, 0)),
                  pl.BlockSpec((1, d), lambda i: (0, 0)), resident((d, d)), resident((pd, d)),
                  pl.BlockSpec((1, d), lambda i: (0, 0))],
        out_specs=pl.BlockSpec((tm, d), lambda i: (i, 0)),
        out_shape=jax.ShapeDtypeStruct((s, d), F32),
        scratch_shapes=[pltpu.VMEM((d, d), BF16), pltpu.VMEM((pd, d), BF16)],
        compiler_params=_params(("arbitrary",)),
        name="ple",
    )(x2, p2d, ple_norm.reshape(1, d), w_gate, w_proj, norm_final.reshape(1, d))


def kernel(x, p, norm_mix, w_in, gm_v_norm, gm_w_s, gm_b_s, dn_conv_w, dn_a_log, dn_dt_bias, dn_out_norm,
           w_branch_a, w_branch_b, w_out, norm_ffn, w_gate_up, w_down, ple_norm, w_ple_gate, w_ple_proj,
           norm_final):
    b, s, d = x.shape
    depth = w_in.shape[0]
    gm_w = w_branch_a.shape[1]
    dn_w = w_branch_b.shape[1]
    heads = dn_a_log.shape[1]
    c_uv, c_qkv, c_z = 0, 2 * gm_w, 2 * gm_w + 3 * dn_w
    c_ba = c_z + dn_w
    c_gate = c_ba + 2 * heads
    xs = x.reshape(b * s, d)

    def layer(w, i):
        return w.reshape(w.shape[1:]) if depth == 1 else w[i]

    def layer_bf16(w, i):
        return layer(w.astype(BF16), i)

    for i in range(depth):
        w_t = layer(jnp.swapaxes(w_in, 1, 2), i)
        hb16, col, row = _norm_small(xs, norm_mix[i], w_t[c_ba:c_gate], dn_a_log[i], dn_dt_bias[i])
        y_a = _gmlp(hb16, w_t, gm_v_norm[i], gm_w_s[i], gm_b_s[i])
        big = _proj(hb16, w_t, dn_conv_w[i], row0=c_qkv, gate_row0=c_gate, dn_w=dn_w, seq=s)
        y_b = _delta(big, col, row, dn_out_norm[i], width=dn_w, seq=s)
        merged = _merge(y_a, layer_bf16(w_branch_a, i), y_b, layer_bf16(w_branch_b, i), big, gate_col0=4 * dn_w)
        x1 = _resid_mm(merged, layer(w_out, i), xs, tm=1024, tn=1024, name="out_proj")
        f = _ffn_up(x1, norm_ffn[i], layer_bf16(w_gate_up, i))
        x2 = _resid_mm(f, layer_bf16(w_down, i), x1, tm=512, tn=1024, name="ffn_down")
        xs = _ple(x2, p[i].reshape(b * s, -1), ple_norm[i], layer(w_ple_gate, i),
                  layer(w_ple_proj, i), norm_final, final_norm=(i == depth - 1))
    return xs.reshape(b, s, d)
```

```python
import functools

import jax
import jax.numpy as jnp
from jax import lax
from jax.experimental import pallas as pl
from jax.experimental.pallas import tpu as pltpu

EPS = 1e-6
GM_CHUNK = 128
GM_GROUP_DIM = 128
DN_HEAD_DIM = 128
DN_CONV = 4
DN_CHUNK = 64
MXU_COLS = 256
V7X_VMEM_LIMIT = 56 * 1024 * 1024

F32 = jnp.float32
BF16 = jnp.bfloat16


def _mm(a, b):
    return jnp.dot(a.astype(BF16), b.astype(BF16), preferred_element_type=F32)


def _mm_nt(a, b):
    return lax.dot_general(a.astype(BF16), b.astype(BF16), (((1,), (1,)), ((), ())),
                           preferred_element_type=F32)


def _mm_tn(a, b):
    return lax.dot_general(a.astype(BF16), b.astype(BF16), (((0,), (0,)), ((), ())),
                           preferred_element_type=F32)


def _rms(xf, gain):
    return xf * lax.rsqrt(jnp.mean(xf * xf, axis=-1, keepdims=True) + EPS) * gain


def _sigmoid(x):
    return 1.0 / (1.0 + jnp.exp(-x))


def _softplus(x):
    return jnp.maximum(x, 0.0) + jnp.log1p(jnp.exp(-jnp.abs(x)))


def _split3(x):
    x1 = x.astype(BF16)
    r1 = x - x1.astype(F32)
    x2 = r1.astype(BF16)
    x3 = (r1 - x2.astype(F32)).astype(BF16)
    return x1, x2, x3


def _params(sem, vmem=V7X_VMEM_LIMIT):
    return pltpu.CompilerParams(dimension_semantics=sem, vmem_limit_bytes=vmem)


def _norm_small_kernel(x_ref, gain_ref, wc_ref, wr_ref, alog_c_ref, dtb_c_ref, alog_r_ref, dtb_r_ref,
                       h_ref, col_ref, row_ref, *, heads):
    tm = x_ref.shape[0]
    h = _rms(x_ref[...], gain_ref[...]).astype(BF16)
    h_ref[...] = h
    pc = jnp.dot(h, wc_ref[...], preferred_element_type=F32)
    pr = lax.dot_general(wr_ref[...], h, (((1,), (1,)), ((), ())), preferred_element_type=F32)
    beta_c = _sigmoid(pc[:, :heads])
    g_c = -jnp.exp(alog_c_ref[...]) * _softplus(pc[:, heads:] + dtb_c_ref[...])
    beta_r = _sigmoid(pr[:heads, :])
    g_r = -jnp.exp(alog_r_ref[...]) * _softplus(pr[heads:, :] + dtb_r_ref[...])
    ri = lax.broadcasted_iota(jnp.int32, (tm, tm), 0)
    ci = lax.broadcasted_iota(jnp.int32, (tm, tm), 1)
    same = (ri // DN_CHUNK) == (ci // DN_CHUNK)
    ltri = jnp.where(same & (ci <= ri), 1.0, 0.0).astype(BF16)
    utri = jnp.where(same & (ri <= ci), 1.0, 0.0).astype(BF16)
    ones = jnp.where(same, 1.0, 0.0).astype(BF16)

    def left(m, parts):
        return sum(jnp.dot(m, p, preferred_element_type=F32) for p in parts)

    def right(parts, m):
        return sum(jnp.dot(p, m, preferred_element_type=F32) for p in parts)

    cs, rs = _split3(g_c), _split3(g_r)
    gc_c, gl_c = left(ltri, cs), left(ones, cs)
    gc_r, gl_r = right(rs, utri), right(rs, ones)
    col_ref[...] = jnp.concatenate([beta_c, gc_c, jnp.exp(gc_c), jnp.exp(gl_c - gc_c)], axis=1)
    row_ref[...] = jnp.concatenate([beta_r, gc_r, jnp.exp(gl_r)], axis=0)


def _norm_small(x2d, norm_mix, w_ba_t, a_log, dt_bias, *, tm=512):
    s, d = x2d.shape
    heads = a_log.shape[0]
    wr = w_ba_t.astype(BF16)
    wc = jnp.transpose(w_ba_t).astype(BF16)
    full = lambda shape: pl.BlockSpec(shape, lambda i: (0,) * len(shape))
    return pl.pallas_call(
        functools.partial(_norm_small_kernel, heads=heads),
        grid=(s // tm,),
        in_specs=[pl.BlockSpec((tm, d), lambda i: (i, 0)), full((1, d)), full((d, 2 * heads)),
                  full((2 * heads, d)), full((1, heads)), full((1, heads)), full((heads, 1)), full((heads, 1))],
        out_specs=[pl.BlockSpec((tm, d), lambda i: (i, 0)),
                   pl.BlockSpec((tm, 4 * heads), lambda i: (i, 0)), pl.BlockSpec((3 * heads, tm), lambda i: (0, i))],
        out_shape=[jax.ShapeDtypeStruct((s, d), BF16),
                   jax.ShapeDtypeStruct((s, 4 * heads), F32), jax.ShapeDtypeStruct((3 * heads, s), F32)],
        compiler_params=_params(("arbitrary",)),
        name="norm_small",
    )(x2d, norm_mix.reshape(1, d), wc, wr, a_log.reshape(1, heads), dt_bias.reshape(1, heads),
      a_log.reshape(heads, 1), dt_bias.reshape(heads, 1))


def _gmlp_kernel(h_ref, w_ref, vgain_ref, ws_ref, bs_ref, o_ref, wb_ref, *, groups):
    tm = h_ref.shape[0]
    gw = groups * GM_GROUP_DIM

    @pl.when(pl.program_id(0) == 0)
    def _():
        wb_ref[...] = w_ref[...].astype(BF16)

    uv = _mm_nt(h_ref[...], wb_ref[...])
    uv = 0.5 * uv * (1.0 + lax.erf(uv * (2.0 ** -0.5)))
    u = uv[:, :gw]
    v = _rms(uv[:, gw:], vgain_ref[...]).astype(BF16)
    ri = lax.broadcasted_iota(jnp.int32, (GM_CHUNK, GM_CHUNK), 0)
    ci = lax.broadcasted_iota(jnp.int32, (GM_CHUNK, GM_CHUNK), 1)
    causal = ci <= ri
    for g in range(groups):
        wg = jnp.where(causal, ws_ref[g], 0.0).astype(BF16)
        bias = bs_ref[:, g:g + 1]
        cols = slice(g * GM_GROUP_DIM, (g + 1) * GM_GROUP_DIM)
        for c in range(tm // GM_CHUNK):
            rows = slice(c * GM_CHUNK, (c + 1) * GM_CHUNK)
            sv = jnp.dot(wg, v[rows, cols], preferred_element_type=F32) + bias
            o_ref[rows, cols] = (u[rows, cols] * sv).astype(o_ref.dtype)


def _gmlp(hb, w_uv, v_gain, w_s, b_s, *, tm=512):
    s, d = hb.shape
    groups = w_s.shape[0]
    gw = groups * GM_GROUP_DIM
    return pl.pallas_call(
        functools.partial(_gmlp_kernel, groups=groups),
        grid=(s // tm,),
        in_specs=[pl.BlockSpec((tm, d), lambda i: (i, 0)),
                  pl.BlockSpec((2 * gw, d), lambda i: (0, 0), pipeline_mode=pl.Buffered(1)),
                  pl.BlockSpec((1, gw), lambda i: (0, 0)),
                  pl.BlockSpec((groups, GM_CHUNK, GM_CHUNK), lambda i: (0, 0, 0)),
                  pl.BlockSpec((GM_CHUNK, groups), lambda i: (0, 0))],
        out_specs=pl.BlockSpec((tm, gw), lambda i: (i, 0)),
        out_shape=jax.ShapeDtypeStruct((s, gw), BF16),
        scratch_shapes=[pltpu.VMEM((2 * gw, d), BF16)],
        compiler_params=_params(("arbitrary",)),
        name="gmlp",
    )(hb, w_uv, v_gain.reshape(1, gw), w_s, jnp.transpose(b_s))


def _proj_kernel(h_ref, w_ref, wx_ref, cw_ref, mode_ref, o_ref, wb_ref, tail_ref, cbuf_ref, *, nw, seq_tiles):
    j = pl.program_id(0)
    i = pl.program_id(1)
    tm, tn = o_ref.shape
    dh = DN_HEAD_DIM
    shift = wx_ref.shape[0]

    @pl.when((i == 0) & (j < 4 * nw))
    def _():
        wb_ref[...] = w_ref[...].astype(BF16)

    @pl.when((i == 0) & (j >= 4 * nw))
    def _():
        wb_ref[0:tn - shift, :] = w_ref[shift:tn, :].astype(BF16)
        wb_ref[tn - shift:tn, :] = wx_ref[...].astype(BF16)

    sub = cbuf_ref.shape[1]
    blocks = [slice(b * sub, (b + 1) * sub) for b in range(tn // sub)]

    def pipelined(finish):
        pending = None
        for cols in blocks:
            a = _mm_nt(h_ref[...], wb_ref[cols, :])
            if pending is not None:
                finish(*pending)
            pending = (cols, a)
        finish(*pending)

    def silu(y):
        hy = 0.5 * y
        return hy + hy * jnp.tanh(hy)

    def conv_silu(cols, a):
        cbuf_ref[0:8, :] = tail_ref[:, cols]
        cbuf_ref[8:8 + tm, :] = a
        w = cw_ref[:, cols]
        y = a * w[DN_CONV - 1:DN_CONV, :]
        for jj in range(DN_CONV - 1):
            off = 8 - (DN_CONV - 1) + jj
            y = y + cbuf_ref[off:off + tm, :] * w[jj:jj + 1, :]
        tail_ref[:, cols] = a[tm - 8:, :]
        return silu(y)

    silu_w = mode_ref[0, 0:1, 0:1]

    def qk_finish(cols, a):
        y = conv_silu(cols, a)
        for h in range(sub // dh):
            yh = y[:, h * dh:(h + 1) * dh]
            inv = lax.rsqrt(jnp.sum(yh * yh, axis=-1, keepdims=True) + EPS)
            o_ref[:, cols.start + h * dh:cols.start + (h + 1) * dh] = (yh * inv).astype(o_ref.dtype)

    def v_finish(cols, a):
        o_ref[:, cols] = conv_silu(cols, a).astype(o_ref.dtype)

    def gate_finish(cols, a):
        sig = 0.5 + 0.5 * jnp.tanh(0.5 * a)
        o_ref[:, cols] = (sig * (silu_w * a + (1.0 - silu_w))).astype(o_ref.dtype)

    @pl.when((j < 3 * nw) & (i % seq_tiles == 0))
    def _():
        tail_ref[...] = jnp.zeros_like(tail_ref)

    @pl.when(j < 2 * nw)
    def _():
        pipelined(qk_finish)

    @pl.when((j >= 2 * nw) & (j < 3 * nw))
    def _():
        pipelined(v_finish)

    @pl.when(j >= 3 * nw)
    def _():
        pipelined(gate_finish)


def _proj(hb, w_t, conv_w, *, row0, gate_row0, dn_w, seq, tm=1024, tn=1024):
    s, d = hb.shape
    nw = dn_w // tn
    n_main = 4 * nw
    n_cols = 4 * dn_w + (w_t.shape[0] - gate_row0)
    n_tiles = n_cols // tn
    sw = [0.0] * (3 * nw) + [1.0] * nw + [0.0] * (n_tiles - 4 * nw)
    mode = jnp.broadcast_to(jnp.asarray(sw, F32)[:, None, None], (n_tiles, 8, 128))
    g_blk, shift = divmod(gate_row0, tn)
    assert row0 % tn == 0 and shift % 16 == 0 and 0 < shift and tn % shift == 0
    per = tn // shift

    def w_main(j, i):
        return jnp.where(j < n_main, row0 // tn + j, g_blk + j - n_main), 0

    def w_extra(j, i):
        return (g_blk + 1 + jnp.maximum(j - n_main, 0)) * per, 0

    return pl.pallas_call(
        functools.partial(_proj_kernel, nw=nw, seq_tiles=seq // tm),
        grid=(n_cols // tn, s // tm),
        in_specs=[pl.BlockSpec((tm, d), lambda j, i: (i, 0)),
                  pl.BlockSpec((tn, d), w_main),
                  pl.BlockSpec((shift, d), w_extra),
                  pl.BlockSpec((DN_CONV, tn), lambda j, i: (0, jnp.minimum(j, 3 * nw - 1))),
                  pl.BlockSpec((1, 8, 128), lambda j, i: (j, 0, 0))],
        out_specs=pl.BlockSpec((tm, tn), lambda j, i: (i, j)),
        out_shape=jax.ShapeDtypeStruct((s, n_cols), BF16),
        scratch_shapes=[pltpu.VMEM((tn, d), BF16), pltpu.VMEM((8, tn), F32), pltpu.VMEM((tm + 8, MXU_COLS), F32)],
        compiler_params=_params(("arbitrary", "arbitrary")),
        name="proj",
    )(hb, w_t, w_t, conv_w, mode)


def _delta_kernel(q_ref, k_ref, v_ref, z_ref, col_ref, row_ref, onorm_ref, o_ref,
                  state_ref, wq_s, uc_s, qk_s, kb_s, eg_s, et_s, el_s, *, heads, seq_tiles, group, stride):
    tile = q_ref.shape[0]
    c = DN_CHUNK
    dh = DN_HEAD_DIM
    t = pl.program_id(0)
    wslot = t % 2
    rslot = (t + 1) % 2

    @pl.when(t == 0)
    def _():
        for ref in (wq_s, uc_s, qk_s, kb_s, eg_s, et_s, el_s):
            ref[...] = jnp.zeros_like(ref)

    @pl.when((t == 0) | ((t - 1) % seq_tiles == 0))
    def _():
        state_ref[...] = jnp.zeros_like(state_ref)

    nch = tile // c
    assert nch == 2
    ri = lax.broadcasted_iota(jnp.int32, (c, 2 * c), 0)
    li = lax.broadcasted_iota(jnp.int32, (c, 2 * c), 1)
    ci = li % c
    left_half = li < c
    tril = ci <= ri
    strict = ci < ri
    blk16 = (ri // 16) == (ci // 16)
    blk32 = (ri // 32) == (ci // 32)
    eye = jnp.where(ri == ci, 1.0, 0.0)
    onorm = onorm_ref[...]
    zero_b = jnp.zeros((c, dh), BF16)
    q_scale = dh ** -0.5

    def rows_of(n):
        return slice(n * c, (n + 1) * c)

    def lanes_of(i):
        return slice(i * dh, (i + 1) * dh)

    def recurrence():
        state = [state_ref[i] for i in range(heads)]
        for n in range(nch):
            us = [i * nch + n for i in range(heads)]
            ws_qs = [jnp.dot(wq_s[rslot, u], state[i].astype(BF16), preferred_element_type=F32)
                     for i, u in enumerate(us)]
            yield
            v_new = [uc_s[rslot, u] - ws_qs[i][:c] for i, u in enumerate(us)]
            pad = (lambda v: jnp.concatenate([v, zero_b], axis=0)) if n == 0 else \
                  (lambda v: jnp.concatenate([zero_b, v], axis=0))
            o_in = [jnp.dot(qk_s[rslot, i], pad(v_new[i].astype(BF16)), preferred_element_type=F32)
                    for i in range(heads)]
            kv = [_mm_tn(kb_s[rslot, u], v_new[i] * et_s[rslot, u]) for i, u in enumerate(us)]
            yield
            state = [state[i] * el_s[rslot, u][0:1, :] + kv[i] for i, u in enumerate(us)]
            for i, u in enumerate(us):
                o = ws_qs[i][c:] * eg_s[rslot, u] + o_in[i]
                out = _rms(o, onorm) * z_ref[rows_of(n), lanes_of(i)].astype(F32)
                o_ref[rows_of(n), lanes_of(i)] = out.astype(o_ref.dtype)
            yield
        for i in range(heads):
            state_ref[i] = state[i]

    rec = recurrence()

    def tick():
        next(rec, None)

    col = col_ref[...]
    lane = lax.broadcasted_iota(jnp.int32, col.shape, 1)

    def col_of(section, i):
        return jnp.sum(jnp.where(lane == section * heads + i, col, 0.0), axis=-1, keepdims=True)

    b_col = [col_of(0, i) for i in range(heads)]
    g_col = [col_of(1, i) for i in range(heads)]
    eg_col = [col_of(2, i) for i in range(heads)]
    et_col = [col_of(3, i) for i in range(heads)]
    b_row = [row_ref[i:i + 1, :] for i in range(heads)]
    g_row = [row_ref[heads + i:heads + i + 1, :] for i in range(heads)]
    el_row = [row_ref[2 * heads + i:2 * heads + i + 1, :] for i in range(heads)]
    tick()

    chunk = lambda ref, i, n: ref[rows_of(n), lanes_of(i)]

    def block_diag(a, b):
        return jnp.concatenate([jnp.concatenate([a, zero_b], axis=1), jnp.concatenate([zero_b, b], axis=1)], axis=0)

    def packed_diag(y):
        yb = y.astype(BF16)
        zero = jnp.zeros_like(yb)
        return jnp.concatenate([jnp.where(left_half, yb, zero), jnp.where(left_half, zero, yb)], axis=0)

    def col_packed(cols, i):
        return jnp.where(left_half, cols[i][rows_of(0)], cols[i][rows_of(1)])

    def prep(hs):
        k0, k1 = {i: chunk(k_ref, i, 0) for i in hs}, {i: chunk(k_ref, i, 1) for i in hs}
        q0, q1 = {i: chunk(q_ref, i, 0) for i in hs}, {i: chunk(q_ref, i, 1) for i in hs}
        for i in hs:
            for n, (kk, qq) in enumerate(((k0[i], q0[i]), (k1[i], q1[i]))):
                u = i * nch + n
                kb_s[wslot, u] = kk
                wq_s[wslot, u, c:2 * c, :] = qq
                eg_s[wslot, u] = jnp.broadcast_to(eg_col[i][rows_of(n)] * q_scale, (c, dh))
                et_s[wslot, u] = jnp.broadcast_to(et_col[i][rows_of(n)], (c, dh))
                el_s[wslot, u] = jnp.broadcast_to(el_row[i][:, n * c:n * c + 1], (8, dh))
        yield
        e = {i: jnp.exp(col_packed(g_col, i) - g_row[i]) for i in hs}
        kd = {i: block_diag(k0[i], k1[i]) for i in hs}
        kq = {i: _mm_nt(jnp.concatenate([jnp.concatenate([k0[i], k1[i]], axis=1),
                                         jnp.concatenate([q0[i], q1[i]], axis=1)], axis=0), kd[i]) for i in hs}
        yield
        for i in hs:
            qk_s[wslot, i] = jnp.where(tril, kq[i][c:] * (e[i] * q_scale), 0.0).astype(BF16)
        nm = {i: jnp.where(strict, kq[i][:c] * (e[i] * (-col_packed(b_col, i))), 0.0) for i in hs}
        dg = {i: jnp.where(blk16, nm[i], 0.0) for i in hs}
        x = {i: eye + dg[i] for i in hs}
        dk = {i: _mm(dg[i], packed_diag(dg[i])).astype(BF16) for i in hs}
        yield
        for _ in range(2):
            r = {i: _mm(jnp.concatenate([x[i].astype(BF16), dk[i]], axis=0), packed_diag(dk[i])) for i in hs}
            x = {i: x[i] + r[i][:c] for i in hs}
            dk = {i: r[i][c:].astype(BF16) for i in hs}
            yield
        x = {i: x[i] + _mm(x[i], packed_diag(dk[i])) for i in hs}
        yield
        for level in (0, 1):
            if level == 0:
                off = {i: jnp.where(blk32 & jnp.logical_not(blk16), nm[i], 0.0) for i in hs}
            else:
                off = {i: jnp.where(blk32, 0.0, nm[i]) for i in hs}
            y = {i: _mm(off[i], packed_diag(x[i])) for i in hs}
            yield
            x = {i: x[i] + _mm(x[i], packed_diag(y[i])) for i in hs}
            yield
        for i in hs:
            b_r, g_r = b_row[i], g_row[i]
            vd = block_diag(chunk(v_ref, i, 0), chunk(v_ref, i, 1))
            uc = _mm(x[i] * b_r, vd)
            wc = _mm(x[i] * (b_r * jnp.exp(g_r)), kd[i])
            for n in range(nch):
                uc_s[wslot, i * nch + n] = uc[:, n * dh:(n + 1) * dh]
                wq_s[wslot, i * nch + n, 0:c, :] = wc[:, n * dh:(n + 1) * dh].astype(BF16)
        yield

    stage = 0
    for g0 in range(0, heads, group):
        for _ in prep(range(g0, min(g0 + group, heads))):
            stage += 1
            if stage % stride == 0:
                tick()
    for _ in rec:
        pass


def _delta(big, col, row, out_norm, *, width, seq, tile=128, group=16, stride=1):
    s = big.shape[0]
    heads = width // DN_HEAD_DIM
    c, dh = DN_CHUNK, DN_HEAD_DIM
    nt = s // tile
    nu = heads * (tile // c)
    cur = lambda sec: pl.BlockSpec((tile, width), lambda t: (jnp.minimum(t, nt - 1), sec))
    prev = lambda sec: pl.BlockSpec((tile, width), lambda t: (jnp.maximum(t - 1, 0), sec))
    return pl.pallas_call(
        functools.partial(_delta_kernel, heads=heads, seq_tiles=seq // tile, group=group, stride=stride),
        grid=(nt + 1,),
        in_specs=[cur(0), cur(1), cur(2), prev(3),
                  pl.BlockSpec((tile, col.shape[1]), lambda t: (jnp.minimum(t, nt - 1), 0)),
                  pl.BlockSpec((row.shape[0], tile), lambda t: (0, jnp.minimum(t, nt - 1))),
                  pl.BlockSpec((1, dh), lambda t: (0, 0))],
        out_specs=prev(0),
        out_shape=jax.ShapeDtypeStruct((s, width), BF16),
        scratch_shapes=[pltpu.VMEM((heads, dh, dh), F32),
                        pltpu.VMEM((2, nu, 2 * c, dh), BF16),
                        pltpu.VMEM((2, nu, c, dh), F32),
                        pltpu.VMEM((2, heads, c, 2 * c), BF16),
                        pltpu.VMEM((2, nu, c, dh), BF16),
                        pltpu.VMEM((2, nu, c, dh), F32),
                        pltpu.VMEM((2, nu, c, dh), F32),
                        pltpu.VMEM((2, nu, 8, dh), F32)],
        compiler_params=_params(("arbitrary",)),
        name="delta",
    )(big, big, big, big, col, row, out_norm.reshape(1, dh))


def _merge_kernel(ya_ref, wa_ref, yb_ref, wb_ref, ga_ref, gb_ref, o_ref):
    a = jnp.dot(ya_ref[...], wa_ref[...], preferred_element_type=F32)
    b = jnp.dot(yb_ref[...], wb_ref[...], preferred_element_type=F32)
    o_ref[...] = (ga_ref[...].astype(F32) * a + gb_ref[...].astype(F32) * b).astype(o_ref.dtype)


def _merge(ya, wa, yb, wb, big, *, gate_col0, tm=1024, tn=1024):
    s, ka = ya.shape
    kb = yb.shape[1]
    n = wa.shape[1]
    ga0 = gate_col0 // tn
    gb0 = (gate_col0 + n) // tn
    return pl.pallas_call(
        _merge_kernel,
        grid=(s // tm, n // tn),
        in_specs=[pl.BlockSpec((tm, ka), lambda i, j: (i, 0)), pl.BlockSpec((ka, tn), lambda i, j: (0, j)),
                  pl.BlockSpec((tm, kb), lambda i, j: (i, 0)), pl.BlockSpec((kb, tn), lambda i, j: (0, j)),
                  pl.BlockSpec((tm, tn), lambda i, j: (i, ga0 + j)), pl.BlockSpec((tm, tn), lambda i, j: (i, gb0 + j))],
        out_specs=pl.BlockSpec((tm, tn), lambda i, j: (i, j)),
        out_shape=jax.ShapeDtypeStruct((s, n), BF16),
        compiler_params=_params(("arbitrary", "arbitrary")),
        name="merge",
    )(ya, wa, yb, wb, big, big)


def _resid_mm_kernel(a_ref, w_ref, r_ref, o_ref, *wb_ref):
    if wb_ref:
        @pl.when(pl.program_id(1) == 0)
        def _():
            wb_ref[0][...] = w_ref[...].astype(BF16)
        w = wb_ref[0][...]
    else:
        w = w_ref[...]
    o_ref[...] = r_ref[...] + jnp.dot(a_ref[...], w, preferred_element_type=F32)


def _resid_mm(a, w, resid, *, tm, tn, name):
    s, k = a.shape
    n = w.shape[1]
    return pl.pallas_call(
        _resid_mm_kernel,
        grid=(n // tn, s // tm),
        in_specs=[pl.BlockSpec((tm, k), lambda j, i: (i, 0)), pl.BlockSpec((k, tn), lambda j, i: (0, j)),
                  pl.BlockSpec((tm, tn), lambda j, i: (i, j))],
        out_specs=pl.BlockSpec((tm, tn), lambda j, i: (i, j)),
        out_shape=jax.ShapeDtypeStruct((s, n), F32),
        scratch_shapes=[] if w.dtype == BF16 else [pltpu.VMEM((k, tn), BF16)],
        compiler_params=_params(("arbitrary", "arbitrary")),
        name=name,
    )(a, w, resid)


def _out_proj_kernel(a_ref, w_ref, r_ref, gain_ref, x_ref, h_ref, wb_ref):
    @pl.when(pl.program_id(0) == 0)
    def _():
        wb_ref[...] = w_ref[...].astype(BF16)

    x1 = r_ref[...] + jnp.dot(a_ref[...], wb_ref[...], preferred_element_type=F32)
    x_ref[...] = x1
    h_ref[...] = _rms(x1, gain_ref[...]).astype(h_ref.dtype)


def _out_proj(merged, w_out, resid, gain, *, tm=512):
    s, k = merged.shape
    d = w_out.shape[1]
    return pl.pallas_call(
        _out_proj_kernel,
        grid=(s // tm,),
        in_specs=[pl.BlockSpec((tm, k), lambda i: (i, 0)),
                  pl.BlockSpec((k, d), lambda i: (0, 0), pipeline_mode=pl.Buffered(1)),
                  pl.BlockSpec((tm, d), lambda i: (i, 0)), pl.BlockSpec((1, d), lambda i: (0, 0))],
        out_specs=[pl.BlockSpec((tm, d), lambda i: (i, 0)), pl.BlockSpec((tm, d), lambda i: (i, 0))],
        out_shape=[jax.ShapeDtypeStruct((s, d), F32), jax.ShapeDtypeStruct((s, d), BF16)],
        scratch_shapes=[pltpu.VMEM((k, d), BF16)],
        compiler_params=_params(("arbitrary",)),
        name="out_proj",
    )(merged, w_out, resid, gain.reshape(1, d))


def _ffn_up_kernel(h_ref, wg_ref, wu_ref, o_ref, wgb_ref, wub_ref):
    @pl.when(pl.program_id(1) == 0)
    def _():
        wgb_ref[...] = wg_ref[...].astype(BF16)
        wub_ref[...] = wu_ref[...].astype(BF16)

    def finish(cols, g, u):
        hg = 0.5 * g
        o_ref[:, cols] = ((hg + hg * jnp.tanh(hg)) * u).astype(o_ref.dtype)

    pending = None
    for b in range(o_ref.shape[1] // MXU_COLS):
        cols = slice(b * MXU_COLS, (b + 1) * MXU_COLS)
        g = jnp.dot(h_ref[...], wgb_ref[:, cols], preferred_element_type=F32)
        u = jnp.dot(h_ref[...], wub_ref[:, cols], preferred_element_type=F32)
        if pending is not None:
            finish(*pending)
        pending = (cols, g, u)
    finish(*pending)


def _ffn_up(h2, w_gate_up, *, tm=1024, tn=512):
    s, d = h2.shape
    dff = w_gate_up.shape[1] // 2
    nj = dff // tn
    return pl.pallas_call(
        _ffn_up_kernel,
        grid=(nj, s // tm),
        in_specs=[pl.BlockSpec((tm, d), lambda j, i: (i, 0)),
                  pl.BlockSpec((d, tn), lambda j, i: (0, j)), pl.BlockSpec((d, tn), lambda j, i: (0, nj + j))],
        out_specs=pl.BlockSpec((tm, tn), lambda j, i: (i, j)),
        out_shape=jax.ShapeDtypeStruct((s, dff), BF16),
        scratch_shapes=[pltpu.VMEM((d, tn), BF16), pltpu.VMEM((d, tn), BF16)],
        compiler_params=_params(("arbitrary", "arbitrary")),
        name="ffn_up",
    )(h2, w_gate_up, w_gate_up)


def _ple_kernel(x_ref, p_ref, pgain_ref, wg_ref, wp_ref, fgain_ref, o_ref, wgb_ref, wpb_ref, *, final_norm):
    @pl.when(pl.program_id(0) == 0)
    def _():
        wgb_ref[...] = wg_ref[...].astype(BF16)
        wpb_ref[...] = wp_ref[...].astype(BF16)

    x = x_ref[...]
    gate = _sigmoid(jnp.dot(_rms(x, pgain_ref[...]).astype(BF16), wgb_ref[...], preferred_element_type=F32))
    proj = jnp.dot(p_ref[...].astype(BF16), wpb_ref[...], preferred_element_type=F32)
    y = x + gate * proj
    o_ref[...] = _rms(y, fgain_ref[...]) if final_norm else y


def _ple(x2, p2d, ple_norm, w_gate, w_proj, norm_final, *, final_norm, tm=512):
    s, d = x2.shape
    pd = p2d.shape[1]
    resident = lambda shape: pl.BlockSpec(shape, lambda i: (0, 0), pipeline_mode=pl.Buffered(1))
    return pl.pallas_call(
        functools.partial(_ple_kernel, final_norm=final_norm),
        grid=(s // tm,),
        in_specs=[pl.BlockSpec((tm, d), lambda i: (i, 0)), pl.BlockSpec((tm, pd), lambda i: (i, 0)),
                  pl.BlockSpec((1, d), lambda i: (0, 0)), resident((d, d)), resident((pd, d)),
                  pl.BlockSpec((1, d), lambda i: (0, 0))],
        out_specs=pl.BlockSpec((tm, d), lambda i: (i, 0)),
        out_shape=jax.ShapeDtypeStruct((s, d), F32),
        scratch_shapes=[pltpu.VMEM((d, d), BF16), pltpu.VMEM((pd, d), BF16)],
        compiler_params=_params(("arbitrary",)),
        name="ple",
    )(x2, p2d, ple_norm.reshape(1, d), w_gate, w_proj, norm_final.reshape(1, d))


def kernel(x, p, norm_mix, w_in, gm_v_norm, gm_w_s, gm_b_s, dn_conv_w, dn_a_log, dn_dt_bias, dn_out_norm,
           w_branch_a, w_branch_b, w_out, norm_ffn, w_gate_up, w_down, ple_norm, w_ple_gate, w_ple_proj,
           norm_final):
    b, s, d = x.shape
    depth = w_in.shape[0]
    gm_w = w_branch_a.shape[1]
    dn_w = w_branch_b.shape[1]
    heads = dn_a_log.shape[1]
    c_uv, c_qkv, c_z = 0, 2 * gm_w, 2 * gm_w + 3 * dn_w
    c_ba = c_z + dn_w
    c_gate = c_ba + 2 * heads
    xs = x.reshape(b * s, d)

    def layer(w, i):
        return w.reshape(w.shape[1:]) if depth == 1 else w[i]

    def layer_bf16(w, i):
        return layer(w.astype(BF16), i)

    for i in range(depth):
        w_t = layer(jnp.swapaxes(w_in, 1, 2), i)
        hb16, col, row = _norm_small(xs, norm_mix[i], w_t[c_ba:c_gate], dn_a_log[i], dn_dt_bias[i])
        y_a = _gmlp(hb16, w_t, gm_v_norm[i], gm_w_s[i], gm_b_s[i])
        big = _proj(hb16, w_t, dn_conv_w[i], row0=c_qkv, gate_row0=c_gate, dn_w=dn_w, seq=s)
        y_b = _delta(big, col, row, dn_out_norm[i], width=dn_w, seq=s)
        merged = _merge(y_a, layer_bf16(w_branch_a, i), y_b, layer_bf16(w_branch_b, i), big, gate_col0=4 * dn_w)
        x1, h2 = _out_proj(merged, layer(w_out, i), xs, norm_ffn[i])
        f = _ffn_up(h2, layer(w_gate_up, i))
        x2 = _resid_mm(f, layer_bf16(w_down, i), x1, tm=512, tn=1024, name="ffn_down")
        xs = _ple(x2, p[i].reshape(b * s, -1), ple_norm[i], layer(w_ple_gate, i),
                  layer(w_ple_proj, i), norm_final, final_norm=(i == depth - 1))
    return xs.reshape(b, s, d)
```

```python
import functools

import jax
import jax.numpy as jnp
from jax import lax
from jax.experimental import pallas as pl
from jax.experimental.pallas import tpu as pltpu

EPS = 1e-6
GM_CHUNK = 128
GM_GROUP_DIM = 128
DN_HEAD_DIM = 128
DN_CONV = 4
DN_CHUNK = 64
MXU_COLS = 256
V7X_VMEM_LIMIT = 56 * 1024 * 1024

F32 = jnp.float32
BF16 = jnp.bfloat16


def _mm(a, b):
    return jnp.dot(a.astype(BF16), b.astype(BF16), preferred_element_type=F32)


def _mm_nt(a, b):
    return lax.dot_general(a.astype(BF16), b.astype(BF16), (((1,), (1,)), ((), ())),
                           preferred_element_type=F32)


def _mm_tn(a, b):
    return lax.dot_general(a.astype(BF16), b.astype(BF16), (((0,), (0,)), ((), ())),
                           preferred_element_type=F32)


def _rms(xf, gain):
    return xf * lax.rsqrt(jnp.mean(xf * xf, axis=-1, keepdims=True) + EPS) * gain


def _sigmoid(x):
    return 1.0 / (1.0 + jnp.exp(-x))


def _softplus(x):
    return jnp.maximum(x, 0.0) + jnp.log1p(jnp.exp(-jnp.abs(x)))


def _split3(x):
    x1 = x.astype(BF16)
    r1 = x - x1.astype(F32)
    x2 = r1.astype(BF16)
    x3 = (r1 - x2.astype(F32)).astype(BF16)
    return x1, x2, x3


def _params(sem, vmem=V7X_VMEM_LIMIT):
    return pltpu.CompilerParams(dimension_semantics=sem, vmem_limit_bytes=vmem)


def _norm_small_kernel(x_ref, gain_ref, wc_ref, wr_ref, alog_c_ref, dtb_c_ref, alog_r_ref, dtb_r_ref,
                       h_ref, col_ref, row_ref, *, heads):
    tm = x_ref.shape[0]
    h = _rms(x_ref[...], gain_ref[...]).astype(BF16)
    h_ref[...] = h
    pc = jnp.dot(h, wc_ref[...], preferred_element_type=F32)
    pr = lax.dot_general(wr_ref[...], h, (((1,), (1,)), ((), ())), preferred_element_type=F32)
    beta_c = _sigmoid(pc[:, :heads])
    g_c = -jnp.exp(alog_c_ref[...]) * _softplus(pc[:, heads:] + dtb_c_ref[...])
    beta_r = _sigmoid(pr[:heads, :])
    g_r = -jnp.exp(alog_r_ref[...]) * _softplus(pr[heads:, :] + dtb_r_ref[...])
    ri = lax.broadcasted_iota(jnp.int32, (tm, tm), 0)
    ci = lax.broadcasted_iota(jnp.int32, (tm, tm), 1)
    same = (ri // DN_CHUNK) == (ci // DN_CHUNK)
    ltri = jnp.where(same & (ci <= ri), 1.0, 0.0).astype(BF16)
    utri = jnp.where(same & (ri <= ci), 1.0, 0.0).astype(BF16)
    ones = jnp.where(same, 1.0, 0.0).astype(BF16)

    def left(m, parts):
        return sum(jnp.dot(m, p, preferred_element_type=F32) for p in parts)

    def right(parts, m):
        return sum(jnp.dot(p, m, preferred_element_type=F32) for p in parts)

    cs, rs = _split3(g_c), _split3(g_r)
    gc_c, gl_c = left(ltri, cs), left(ones, cs)
    gc_r, gl_r = right(rs, utri), right(rs, ones)
    col_ref[...] = jnp.concatenate([beta_c, gc_c, jnp.exp(gc_c), jnp.exp(gl_c - gc_c)], axis=1)
    row_ref[...] = jnp.concatenate([beta_r, gc_r, jnp.exp(gl_r)], axis=0)


def _norm_small(x2d, norm_mix, w_ba_t, a_log, dt_bias, *, tm=512):
    s, d = x2d.shape
    heads = a_log.shape[0]
    wr = w_ba_t.astype(BF16)
    wc = jnp.transpose(w_ba_t).astype(BF16)
    full = lambda shape: pl.BlockSpec(shape, lambda i: (0,) * len(shape))
    return pl.pallas_call(
        functools.partial(_norm_small_kernel, heads=heads),
        grid=(s // tm,),
        in_specs=[pl.BlockSpec((tm, d), lambda i: (i, 0)), full((1, d)), full((d, 2 * heads)),
                  full((2 * heads, d)), full((1, heads)), full((1, heads)), full((heads, 1)), full((heads, 1))],
        out_specs=[pl.BlockSpec((tm, d), lambda i: (i, 0)),
                   pl.BlockSpec((tm, 4 * heads), lambda i: (i, 0)), pl.BlockSpec((3 * heads, tm), lambda i: (0, i))],
        out_shape=[jax.ShapeDtypeStruct((s, d), BF16),
                   jax.ShapeDtypeStruct((s, 4 * heads), F32), jax.ShapeDtypeStruct((3 * heads, s), F32)],
        compiler_params=_params(("arbitrary",)),
        name="norm_small",
    )(x2d, norm_mix.reshape(1, d), wc, wr, a_log.reshape(1, heads), dt_bias.reshape(1, heads),
      a_log.reshape(heads, 1), dt_bias.reshape(heads, 1))


def _gmlp_kernel(h_ref, w_ref, vgain_ref, ws_ref, bs_ref, o_ref, wb_ref, *, groups):
    tm = h_ref.shape[0]
    gw = groups * GM_GROUP_DIM

    @pl.when(pl.program_id(0) == 0)
    def _():
        wb_ref[...] = w_ref[...].astype(BF16)

    uv = _mm_nt(h_ref[...], wb_ref[...])
    uv = 0.5 * uv * (1.0 + lax.erf(uv * (2.0 ** -0.5)))
    u = uv[:, :gw]
    v = _rms(uv[:, gw:], vgain_ref[...]).astype(BF16)
    ri = lax.broadcasted_iota(jnp.int32, (GM_CHUNK, GM_CHUNK), 0)
    ci = lax.broadcasted_iota(jnp.int32, (GM_CHUNK, GM_CHUNK), 1)
    causal = ci <= ri
    for g in range(groups):
        wg = jnp.where(causal, ws_ref[g], 0.0).astype(BF16)
        bias = bs_ref[:, g:g + 1]
        cols = slice(g * GM_GROUP_DIM, (g + 1) * GM_GROUP_DIM)
        for c in range(tm // GM_CHUNK):
            rows = slice(c * GM_CHUNK, (c + 1) * GM_CHUNK)
            sv = jnp.dot(wg, v[rows, cols], preferred_element_type=F32) + bias
            o_ref[rows, cols] = (u[rows, cols] * sv).astype(o_ref.dtype)


def _gmlp(hb, w_uv, v_gain, w_s, b_s, *, tm=512):
    s, d = hb.shape
    groups = w_s.shape[0]
    gw = groups * GM_GROUP_DIM
    return pl.pallas_call(
        functools.partial(_gmlp_kernel, groups=groups),
        grid=(s // tm,),
        in_specs=[pl.BlockSpec((tm, d), lambda i: (i, 0)),
                  pl.BlockSpec((2 * gw, d), lambda i: (0, 0), pipeline_mode=pl.Buffered(1)),
                  pl.BlockSpec((1, gw), lambda i: (0, 0)),
                  pl.BlockSpec((groups, GM_CHUNK, GM_CHUNK), lambda i: (0, 0, 0)),
                  pl.BlockSpec((GM_CHUNK, groups), lambda i: (0, 0))],
        out_specs=pl.BlockSpec((tm, gw), lambda i: (i, 0)),
        out_shape=jax.ShapeDtypeStruct((s, gw), BF16),
        scratch_shapes=[pltpu.VMEM((2 * gw, d), BF16)],
        compiler_params=_params(("arbitrary",)),
        name="gmlp",
    )(hb, w_uv, v_gain.reshape(1, gw), w_s, jnp.transpose(b_s))


def _proj_kernel(h_ref, w_ref, wx_ref, cw_ref, mode_ref, o_ref, wb_ref, tail_ref, cbuf_ref, *, nw, seq_tiles):
    j = pl.program_id(0)
    i = pl.program_id(1)
    tm, tn = o_ref.shape
    dh = DN_HEAD_DIM
    shift = wx_ref.shape[0]

    @pl.when((i == 0) & (j < 4 * nw))
    def _():
        wb_ref[...] = w_ref[...].astype(BF16)

    @pl.when((i == 0) & (j >= 4 * nw))
    def _():
        wb_ref[0:tn - shift, :] = w_ref[shift:tn, :].astype(BF16)
        wb_ref[tn - shift:tn, :] = wx_ref[...].astype(BF16)

    sub = cbuf_ref.shape[1]
    blocks = [slice(b * sub, (b + 1) * sub) for b in range(tn // sub)]

    def pipelined(finish):
        pending = None
        for cols in blocks:
            a = _mm_nt(h_ref[...], wb_ref[cols, :])
            if pending is not None:
                finish(*pending)
            pending = (cols, a)
        finish(*pending)

    def silu(y):
        hy = 0.5 * y
        return hy + hy * jnp.tanh(hy)

    def conv_silu(cols, a):
        cbuf_ref[0:8, :] = tail_ref[:, cols]
        cbuf_ref[8:8 + tm, :] = a
        w = cw_ref[:, cols]
        y = a * w[DN_CONV - 1:DN_CONV, :]
        for jj in range(DN_CONV - 1):
            off = 8 - (DN_CONV - 1) + jj
            y = y + cbuf_ref[off:off + tm, :] * w[jj:jj + 1, :]
        tail_ref[:, cols] = a[tm - 8:, :]
        return silu(y)

    silu_w = mode_ref[0, 0:1, 0:1]

    def qk_finish(cols, a):
        y = conv_silu(cols, a)
        for h in range(sub // dh):
            yh = y[:, h * dh:(h + 1) * dh]
            inv = lax.rsqrt(jnp.sum(yh * yh, axis=-1, keepdims=True) + EPS)
            o_ref[:, cols.start + h * dh:cols.start + (h + 1) * dh] = (yh * inv).astype(o_ref.dtype)

    def v_finish(cols, a):
        o_ref[:, cols] = conv_silu(cols, a).astype(o_ref.dtype)

    def gate_finish(cols, a):
        sig = 0.5 + 0.5 * jnp.tanh(0.5 * a)
        o_ref[:, cols] = (sig * (silu_w * a + (1.0 - silu_w))).astype(o_ref.dtype)

    @pl.when((j < 3 * nw) & (i % seq_tiles == 0))
    def _():
        tail_ref[...] = jnp.zeros_like(tail_ref)

    @pl.when(j < 2 * nw)
    def _():
        pipelined(qk_finish)

    @pl.when((j >= 2 * nw) & (j < 3 * nw))
    def _():
        pipelined(v_finish)

    @pl.when(j >= 3 * nw)
    def _():
        pipelined(gate_finish)


def _proj(hb, w_t, conv_w, *, row0, gate_row0, dn_w, seq, tm=1024, tn=1024):
    s, d = hb.shape
    nw = dn_w // tn
    n_main = 4 * nw
    n_cols = 4 * dn_w + (w_t.shape[0] - gate_row0)
    n_tiles = n_cols // tn
    sw = [0.0] * (3 * nw) + [1.0] * nw + [0.0] * (n_tiles - 4 * nw)
    mode = jnp.broadcast_to(jnp.asarray(sw, F32)[:, None, None], (n_tiles, 8, 128))
    g_blk, shift = divmod(gate_row0, tn)
    assert row0 % tn == 0 and shift % 16 == 0 and 0 < shift and tn % shift == 0
    per = tn // shift

    def w_main(j, i):
        return jnp.where(j < n_main, row0 // tn + j, g_blk + j - n_main), 0

    def w_extra(j, i):
        return (g_blk + 1 + jnp.maximum(j - n_main, 0)) * per, 0

    return pl.pallas_call(
        functools.partial(_proj_kernel, nw=nw, seq_tiles=seq // tm),
        grid=(n_cols // tn, s // tm),
        in_specs=[pl.BlockSpec((tm, d), lambda j, i: (i, 0)),
                  pl.BlockSpec((tn, d), w_main),
                  pl.BlockSpec((shift, d), w_extra),
                  pl.BlockSpec((DN_CONV, tn), lambda j, i: (0, jnp.minimum(j, 3 * nw - 1))),
                  pl.BlockSpec((1, 8, 128), lambda j, i: (j, 0, 0))],
        out_specs=pl.BlockSpec((tm, tn), lambda j, i: (i, j)),
        out_shape=jax.ShapeDtypeStruct((s, n_cols), BF16),
        scratch_shapes=[pltpu.VMEM((tn, d), BF16), pltpu.VMEM((8, tn), F32), pltpu.VMEM((tm + 8, MXU_COLS), F32)],
        compiler_params=_params(("arbitrary", "arbitrary")),
        name="proj",
    )(hb, w_t, w_t, conv_w, mode)


def _delta_kernel(q_ref, k_ref, v_ref, z_ref, col_ref, row_ref, onorm_ref, o_ref,
                  state_ref, wq_s, uc_s, qk_s, kb_s, el_s, *, heads, seq_tiles, group, stride):
    tile = q_ref.shape[0]
    c = DN_CHUNK
    dh = DN_HEAD_DIM
    t = pl.program_id(0)
    wslot = t % 2
    rslot = (t + 1) % 2

    @pl.when(t == 0)
    def _():
        for ref in (wq_s, uc_s, qk_s, kb_s, el_s):
            ref[...] = jnp.zeros_like(ref)

    @pl.when((t == 0) | ((t - 1) % seq_tiles == 0))
    def _():
        state_ref[...] = jnp.zeros_like(state_ref)

    nch = tile // c
    assert nch == 2
    ri = lax.broadcasted_iota(jnp.int32, (c, 2 * c), 0)
    li = lax.broadcasted_iota(jnp.int32, (c, 2 * c), 1)
    ci = li % c
    left_half = li < c
    tril = ci <= ri
    strict = ci < ri
    blk16 = (ri // 16) == (ci // 16)
    blk32 = (ri // 32) == (ci // 32)
    eye = jnp.where(ri == ci, 1.0, 0.0)
    onorm = onorm_ref[...]
    zero_b = jnp.zeros((c, dh), BF16)
    q_scale = dh ** -0.5

    def rows_of(n):
        return slice(n * c, (n + 1) * c)

    def lanes_of(i):
        return slice(i * dh, (i + 1) * dh)

    def recurrence():
        state = [state_ref[i] for i in range(heads)]
        for n in range(nch):
            us = [i * nch + n for i in range(heads)]
            ws_qs = [jnp.dot(wq_s[rslot, u], state[i].astype(BF16), preferred_element_type=F32)
                     for i, u in enumerate(us)]
            yield
            v_new = [(uc_s[rslot, u] - ws_qs[i][:c]).astype(BF16) for i, u in enumerate(us)]
            pad = (lambda v: jnp.concatenate([v, zero_b], axis=0)) if n == 0 else \
                  (lambda v: jnp.concatenate([zero_b, v], axis=0))
            o_in = [jnp.dot(qk_s[rslot, i], pad(v_new[i]), preferred_element_type=F32) for i in range(heads)]
            kv = [_mm_tn(kb_s[rslot, u], v_new[i]) for i, u in enumerate(us)]
            yield
            state = [state[i] * el_s[rslot, u][0:1, :] + kv[i] for i, u in enumerate(us)]
            for i, u in enumerate(us):
                o = ws_qs[i][c:] + o_in[i]
                out = _rms(o, onorm) * z_ref[rows_of(n), lanes_of(i)].astype(F32)
                o_ref[rows_of(n), lanes_of(i)] = out.astype(o_ref.dtype)
            yield
        for i in range(heads):
            state_ref[i] = state[i]

    rec = recurrence()

    def tick():
        next(rec, None)

    col = col_ref[...]
    lane = lax.broadcasted_iota(jnp.int32, col.shape, 1)

    def col_of(section, i):
        return jnp.sum(jnp.where(lane == section * heads + i, col, 0.0), axis=-1, keepdims=True)

    b_col = [col_of(0, i) for i in range(heads)]
    g_col = [col_of(1, i) for i in range(heads)]
    eg_col = [col_of(2, i) for i in range(heads)]
    et_col = [col_of(3, i) for i in range(heads)]
    b_row = [row_ref[i:i + 1, :] for i in range(heads)]
    g_row = [row_ref[heads + i:heads + i + 1, :] for i in range(heads)]
    el_row = [row_ref[2 * heads + i:2 * heads + i + 1, :] for i in range(heads)]
    tick()

    chunk = lambda ref, i, n: ref[rows_of(n), lanes_of(i)]

    def block_diag(a, b):
        return jnp.concatenate([jnp.concatenate([a, zero_b], axis=1), jnp.concatenate([zero_b, b], axis=1)], axis=0)

    def packed_diag(y):
        yb = y.astype(BF16)
        zero = jnp.zeros_like(yb)
        return jnp.concatenate([jnp.where(left_half, yb, zero), jnp.where(left_half, zero, yb)], axis=0)

    def col_packed(cols, i):
        return jnp.where(left_half, cols[i][rows_of(0)], cols[i][rows_of(1)])

    def prep(hs):
        k0, k1 = {i: chunk(k_ref, i, 0) for i in hs}, {i: chunk(k_ref, i, 1) for i in hs}
        q0, q1 = {i: chunk(q_ref, i, 0) for i in hs}, {i: chunk(q_ref, i, 1) for i in hs}
        for i in hs:
            for n, (kk, qq) in enumerate(((k0[i], q0[i]), (k1[i], q1[i]))):
                u = i * nch + n
                kb_s[wslot, u] = (kk.astype(F32) * et_col[i][rows_of(n)]).astype(BF16)
                wq_s[wslot, u, c:2 * c, :] = (qq.astype(F32) * (eg_col[i][rows_of(n)] * q_scale)).astype(BF16)
                el_s[wslot, u] = jnp.broadcast_to(el_row[i][:, n * c:n * c + 1], (8, dh))
        yield
        e = {i: jnp.exp(col_packed(g_col, i) - g_row[i]) for i in hs}
        kd = {i: block_diag(k0[i], k1[i]) for i in hs}
        kq = {i: _mm_nt(jnp.concatenate([jnp.concatenate([k0[i], k1[i]], axis=1),
                                         jnp.concatenate([q0[i], q1[i]], axis=1)], axis=0), kd[i]) for i in hs}
        yield
        for i in hs:
            qk_s[wslot, i] = jnp.where(tril, kq[i][c:] * (e[i] * q_scale), 0.0).astype(BF16)
        nm = {i: jnp.where(strict, kq[i][:c] * (e[i] * (-col_packed(b_col, i))), 0.0) for i in hs}
        dg = {i: jnp.where(blk16, nm[i], 0.0) for i in hs}
        x = {i: eye + dg[i] for i in hs}
        dk = {i: _mm(dg[i], packed_diag(dg[i])).astype(BF16) for i in hs}
        yield
        for _ in range(2):
            r = {i: _mm(jnp.concatenate([x[i].astype(BF16), dk[i]], axis=0), packed_diag(dk[i])) for i in hs}
            x = {i: x[i] + r[i][:c] for i in hs}
            dk = {i: r[i][c:].astype(BF16) for i in hs}
            yield
        x = {i: x[i] + _mm(x[i], packed_diag(dk[i])) for i in hs}
        yield
        for level in (0, 1):
            if level == 0:
                off = {i: jnp.where(blk32 & jnp.logical_not(blk16), nm[i], 0.0) for i in hs}
            else:
                off = {i: jnp.where(blk32, 0.0, nm[i]) for i in hs}
            y = {i: _mm(off[i], packed_diag(x[i])) for i in hs}
            yield
            x = {i: x[i] + _mm(x[i], packed_diag(y[i])) for i in hs}
            yield
        for i in hs:
            b_r, g_r = b_row[i], g_row[i]
            vd = block_diag(chunk(v_ref, i, 0), chunk(v_ref, i, 1))
            uc = _mm(x[i] * b_r, vd)
            wc = _mm(x[i] * (b_r * jnp.exp(g_r)), kd[i])
            for n in range(nch):
                uc_s[wslot, i * nch + n] = uc[:, n * dh:(n + 1) * dh]
                wq_s[wslot, i * nch + n, 0:c, :] = wc[:, n * dh:(n + 1) * dh].astype(BF16)
        yield

    stage = 0
    for g0 in range(0, heads, group):
        for _ in prep(range(g0, min(g0 + group, heads))):
            stage += 1
            if stage % stride == 0:
                tick()
    for _ in rec:
        pass


def _delta(big, col, row, out_norm, *, width, seq, tile=128, group=16, stride=1):
    s = big.shape[0]
    heads = width // DN_HEAD_DIM
    c, dh = DN_CHUNK, DN_HEAD_DIM
    nt = s // tile
    nu = heads * (tile // c)
    cur = lambda sec: pl.BlockSpec((tile, width), lambda t: (jnp.minimum(t, nt - 1), sec))
    prev = lambda sec: pl.BlockSpec((tile, width), lambda t: (jnp.maximum(t - 1, 0), sec))
    return pl.pallas_call(
        functools.partial(_delta_kernel, heads=heads, seq_tiles=seq // tile, group=group, stride=stride),
        grid=(nt + 1,),
        in_specs=[cur(0), cur(1), cur(2), prev(3),
                  pl.BlockSpec((tile, col.shape[1]), lambda t: (jnp.minimum(t, nt - 1), 0)),
                  pl.BlockSpec((row.shape[0], tile), lambda t: (0, jnp.minimum(t, nt - 1))),
                  pl.BlockSpec((1, dh), lambda t: (0, 0))],
        out_specs=prev(0),
        out_shape=jax.ShapeDtypeStruct((s, width), BF16),
        scratch_shapes=[pltpu.VMEM((heads, dh, dh), F32),
                        pltpu.VMEM((2, nu, 2 * c, dh), BF16),
                        pltpu.VMEM((2, nu, c, dh), F32),
                        pltpu.VMEM((2, heads, c, 2 * c), BF16),
                        pltpu.VMEM((2, nu, c, dh), BF16),
                        pltpu.VMEM((2, nu, 8, dh), F32)],
        compiler_params=_params(("arbitrary",)),
        name="delta",
    )(big, big, big, big, col, row, out_norm.reshape(1, dh))


def _merge_kernel(ya_ref, wa_ref, yb_ref, wb_ref, ga_ref, gb_ref, o_ref):
    a = jnp.dot(ya_ref[...], wa_ref[...], preferred_element_type=F32)
    b = jnp.dot(yb_ref[...], wb_ref[...], preferred_element_type=F32)
    o_ref[...] = (ga_ref[...].astype(F32) * a + gb_ref[...].astype(F32) * b).astype(o_ref.dtype)


def _merge(ya, wa, yb, wb, big, *, gate_col0, tm=1024, tn=1024):
    s, ka = ya.shape
    kb = yb.shape[1]
    n = wa.shape[1]
    ga0 = gate_col0 // tn
    gb0 = (gate_col0 + n) // tn
    return pl.pallas_call(
        _merge_kernel,
        grid=(s // tm, n // tn),
        in_specs=[pl.BlockSpec((tm, ka), lambda i, j: (i, 0)), pl.BlockSpec((ka, tn), lambda i, j: (0, j)),
                  pl.BlockSpec((tm, kb), lambda i, j: (i, 0)), pl.BlockSpec((kb, tn), lambda i, j: (0, j)),
                  pl.BlockSpec((tm, tn), lambda i, j: (i, ga0 + j)), pl.BlockSpec((tm, tn), lambda i, j: (i, gb0 + j))],
        out_specs=pl.BlockSpec((tm, tn), lambda i, j: (i, j)),
        out_shape=jax.ShapeDtypeStruct((s, n), BF16),
        compiler_params=_params(("arbitrary", "arbitrary")),
        name="merge",
    )(ya, wa, yb, wb, big, big)


def _resid_mm_kernel(a_ref, w_ref, r_ref, o_ref, *wb_ref):
    if wb_ref:
        @pl.when(pl.program_id(1) == 0)
        def _():
            wb_ref[0][...] = w_ref[...].astype(BF16)
        w = wb_ref[0][...]
    else:
        w = w_ref[...]
    o_ref[...] = r_ref[...] + jnp.dot(a_ref[...], w, preferred_element_type=F32)


def _resid_mm(a, w, resid, *, tm, tn, name):
    s, k = a.shape
    n = w.shape[1]
    return pl.pallas_call(
        _resid_mm_kernel,
        grid=(n // tn, s // tm),
        in_specs=[pl.BlockSpec((tm, k), lambda j, i: (i, 0)), pl.BlockSpec((k, tn), lambda j, i: (0, j)),
                  pl.BlockSpec((tm, tn), lambda j, i: (i, j))],
        out_specs=pl.BlockSpec((tm, tn), lambda j, i: (i, j)),
        out_shape=jax.ShapeDtypeStruct((s, n), F32),
        scratch_shapes=[] if w.dtype == BF16 else [pltpu.VMEM((k, tn), BF16)],
        compiler_params=_params(("arbitrary", "arbitrary")),
        name=name,
    )(a, w, resid)


def _out_proj_kernel(a_ref, w_ref, r_ref, gain_ref, x_ref, h_ref, wb_ref):
    @pl.when(pl.program_id(0) == 0)
    def _():
        wb_ref[...] = w_ref[...].astype(BF16)

    x1 = r_ref[...] + jnp.dot(a_ref[...], wb_ref[...], preferred_element_type=F32)
    x_ref[...] = x1
    h_ref[...] = _rms(x1, gain_ref[...]).astype(h_ref.dtype)


def _out_proj(merged, w_out, resid, gain, *, tm=512):
    s, k = merged.shape
    d = w_out.shape[1]
    return pl.pallas_call(
        _out_proj_kernel,
        grid=(s // tm,),
        in_specs=[pl.BlockSpec((tm, k), lambda i: (i, 0)),
                  pl.BlockSpec((k, d), lambda i: (0, 0), pipeline_mode=pl.Buffered(1)),
                  pl.BlockSpec((tm, d), lambda i: (i, 0)), pl.BlockSpec((1, d), lambda i: (0, 0))],
        out_specs=[pl.BlockSpec((tm, d), lambda i: (i, 0)), pl.BlockSpec((tm, d), lambda i: (i, 0))],
        out_shape=[jax.ShapeDtypeStruct((s, d), F32), jax.ShapeDtypeStruct((s, d), BF16)],
        scratch_shapes=[pltpu.VMEM((k, d), BF16)],
        compiler_params=_params(("arbitrary",)),
        name="out_proj",
    )(merged, w_out, resid, gain.reshape(1, d))


def _ffn_up_kernel(h_ref, wg_ref, wu_ref, o_ref, wgb_ref, wub_ref):
    @pl.when(pl.program_id(1) == 0)
    def _():
        wgb_ref[...] = wg_ref[...].astype(BF16)
        wub_ref[...] = wu_ref[...].astype(BF16)

    def finish(cols, g, u):
        hg = 0.5 * g
        o_ref[:, cols] = ((hg + hg * jnp.tanh(hg)) * u).astype(o_ref.dtype)

    pending = None
    for b in range(o_ref.shape[1] // MXU_COLS):
        cols = slice(b * MXU_COLS, (b + 1) * MXU_COLS)
        g = jnp.dot(h_ref[...], wgb_ref[:, cols], preferred_element_type=F32)
        u = jnp.dot(h_ref[...], wub_ref[:, cols], preferred_element_type=F32)
        if pending is not None:
            finish(*pending)
        pending = (cols, g, u)
    finish(*pending)


def _ffn_up(h2, w_gate_up, *, tm=2048, tn=512):
    s, d = h2.shape
    dff = w_gate_up.shape[1] // 2
    nj = dff // tn
    return pl.pallas_call(
        _ffn_up_kernel,
        grid=(nj, s // tm),
        in_specs=[pl.BlockSpec((tm, d), lambda j, i: (i, 0)),
                  pl.BlockSpec((d, tn), lambda j, i: (0, j)), pl.BlockSpec((d, tn), lambda j, i: (0, nj + j))],
        out_specs=pl.BlockSpec((tm, tn), lambda j, i: (i, j)),
        out_shape=jax.ShapeDtypeStruct((s, dff), BF16),
        scratch_shapes=[pltpu.VMEM((d, tn), BF16), pltpu.VMEM((d, tn), BF16)],
        compiler_params=_params(("arbitrary", "arbitrary")),
        name="ffn_up",
    )(h2, w_gate_up, w_gate_up)


def _ple_kernel(x_ref, p_ref, pgain_ref, wg_ref, wp_ref, fgain_ref, o_ref, wgb_ref, wpb_ref, *, final_norm):
    @pl.when(pl.program_id(0) == 0)
    def _():
        wgb_ref[...] = wg_ref[...].astype(BF16)
        wpb_ref[...] = wp_ref[...].astype(BF16)

    x = x_ref[...]
    gate = _sigmoid(jnp.dot(_rms(x, pgain_ref[...]).astype(BF16), wgb_ref[...], preferred_element_type=F32))
    proj = jnp.dot(p_ref[...].astype(BF16), wpb_ref[...], preferred_element_type=F32)
    y = x + gate * proj
    o_ref[...] = _rms(y, fgain_ref[...]) if final_norm else y


def _ple(x2, p2d, ple_norm, w_gate, w_proj, norm_final, *, final_norm, tm=512):
    s, d = x2.shape
    pd = p2d.shape[1]
    resident = lambda shape: pl.BlockSpec(shape, lambda i: (0, 0), pipeline_mode=pl.Buffered(1))
    return pl.pallas_call(
        functools.partial(_ple_kernel, final_norm=final_norm),
        grid=(s // tm,),
        in_specs=[pl.BlockSpec((tm, d), lambda i: (i, 0)), pl.BlockSpec((tm, pd), lambda i: (i, 0)),
                  pl.BlockSpec((1, d), lambda i: (0, 0)), resident((d, d)), resident((pd, d)),
                  pl.BlockSpec((1, d), lambda i: (0, 0))],
        out_specs=pl.BlockSpec((tm, d), lambda i: (i, 0)),
        out_shape=jax.ShapeDtypeStruct((s, d), F32),
        scratch_shapes=[pltpu.VMEM((d, d), BF16), pltpu.VMEM((pd, d), BF16)],
        compiler_params=_params(("arbitrary",)),
        name="ple",
    )(x2, p2d, ple_norm.reshape(1, d), w_gate, w_proj, norm_final.reshape(1, d))


def kernel(x, p, norm_mix, w_in, gm_v_norm, gm_w_s, gm_b_s, dn_conv_w, dn_a_log, dn_dt_bias, dn_out_norm,
           w_branch_a, w_branch_b, w_out, norm_ffn, w_gate_up, w_down, ple_norm, w_ple_gate, w_ple_proj,
           norm_final):
    b, s, d = x.shape
    depth = w_in.shape[0]
    gm_w = w_branch_a.shape[1]
    dn_w = w_branch_b.shape[1]
    heads = dn_a_log.shape[1]
    c_uv, c_qkv, c_z = 0, 2 * gm_w, 2 * gm_w + 3 * dn_w
    c_ba = c_z + dn_w
    c_gate = c_ba + 2 * heads
    xs = x.reshape(b * s, d)

    def layer(w, i):
        return w.reshape(w.shape[1:]) if depth == 1 else w[i]

    def layer_bf16(w, i):
        return layer(w.astype(BF16), i)

    for i in range(depth):
        w_t = layer(jnp.swapaxes(w_in, 1, 2), i)
        hb16, col, row = _norm_small(xs, norm_mix[i], w_t[c_ba:c_gate], dn_a_log[i], dn_dt_bias[i])
        y_a = _gmlp(hb16, w_t, gm_v_norm[i], gm_w_s[i], gm_b_s[i])
        big = _proj(hb16, w_t, dn_conv_w[i], row0=c_qkv, gate_row0=c_gate, dn_w=dn_w, seq=s)
        y_b = _delta(big, col, row, dn_out_norm[i], width=dn_w, seq=s)
        merged = _merge(y_a, layer_bf16(w_branch_a, i), y_b, layer_bf16(w_branch_b, i), big, gate_col0=4 * dn_w)
        x1, h2 = _out_proj(merged, layer(w_out, i), xs, norm_ffn[i])
        f = _ffn_up(h2, layer(w_gate_up, i))
        x2 = _resid_mm(f, layer_bf16(w_down, i), x1, tm=512, tn=1024, name="ffn_down")
        xs = _ple(x2, p[i].reshape(b * s, -1), ple_norm[i], layer(w_ple_gate, i),
                  layer(w_ple_proj, i), norm_final, final_norm=(i == depth - 1))
    return xs.reshape(b, s, d)
```
